```python
import math
import jax, jax.numpy as jnp
from jax import lax
import numpy as np

D_MODEL = 2048
BATCH = 2
SEQ = 4096
DEPTH = 2
DEC_BATCH = 128
DEC_SEQ = 4
PAST_LEN = 2048
PAGE_SIZE = 128

N_A_LAYERS = DEPTH // 2
N_B_LAYERS = DEPTH - N_A_LAYERS
HEAD_DIM = 128
N_HEADS = D_MODEL // HEAD_DIM
CONV_W = 31
D_FF = ((8 * D_MODEL + 3 * 256 - 1) // (3 * 256)) * 256
MOBA_BLOCK = 256
MOBA_TOPK = 3
REL_BUCKETS = 32
REL_MAX_DIST = 128
Q_BLOCK = 128
NORM_EPS = 1e-6

kernel_name = 'yoco_conformer_moba_step'


def rms_norm(x, g):
    xf = x.astype(jnp.float32)
    y = xf * lax.rsqrt(jnp.mean(xf * xf, axis=-1, keepdims=True) + NORM_EPS)
    return (y * g.astype(jnp.float32)).astype(x.dtype)


def layer_norm(x, g, b):
    xf = x.astype(jnp.float32)
    mu = jnp.mean(xf, axis=-1, keepdims=True)
    xc = xf - mu
    y = xc * lax.rsqrt(jnp.mean(xc * xc, axis=-1, keepdims=True) + NORM_EPS)
    return (y * g.astype(jnp.float32) + b.astype(jnp.float32)).astype(x.dtype)


def modulate(h, shift, scale):
    return h * (1 + scale[:, None, :]) + shift[:, None, :]


def split_heads(x):
    return x.reshape(x.shape[:-1] + (N_HEADS, HEAD_DIM))


def swiglu(h, w_gate, w_up, w_down):
    return (jax.nn.silu(h @ w_gate) * (h @ w_up)) @ w_down


def rel_bucket(dist):
    n = jnp.maximum(dist, 0)
    max_exact = REL_BUCKETS // 2
    nf = jnp.maximum(n, 1).astype(jnp.float32)
    large = max_exact + (jnp.log(nf / max_exact) / math.log(REL_MAX_DIST / max_exact)
                         * (REL_BUCKETS - max_exact)).astype(jnp.int32)
    large = jnp.minimum(large, REL_BUCKETS - 1)
    return jnp.where(n < max_exact, n, large)


def conv_module(h, conv_buf, w_in, w_dw, b_dw, ln_g, ln_b, w_out):
    a, g = jnp.split(h @ w_in, 2, axis=-1)
    z = a * jax.nn.sigmoid(g)
    zc = jnp.concatenate([conv_buf.astype(z.dtype), z], axis=1)
    y = lax.conv_general_dilated(zc, w_dw.astype(zc.dtype)[:, None, :], window_strides=(1,),
                                 padding='VALID', dimension_numbers=('NWC', 'WIO', 'NWC'),
                                 feature_group_count=D_MODEL) + b_dw
    y = jax.nn.silu(layer_norm(y, ln_g, ln_b))
    return y @ w_out, zc[:, -(CONV_W - 1):]


def own_block_logits(q, qpos, k_own, kpos_own, rel_bias):
    dist = qpos[:, None] - kpos_own[None, :]
    lg = jnp.einsum('nqhd,nkhd->nqhk', q, k_own.astype(q.dtype), preferred_element_type=jnp.float32)
    lg = lg + rel_bias[rel_bucket(dist)].transpose(0, 2, 1)[None]
    return jnp.where((dist >= 0)[None, :, None, :], lg, -jnp.inf)


def attend_selected(q, qpos, sel_slots, slot_ok, rows_k, rows_v, lg_own, v_own, rel_bias):
    if sel_slots is None:
        p_own = jax.nn.softmax(lg_own, axis=-1)
        return jnp.einsum('nqhk,nkhd->nqhd', p_own.astype(v_own.dtype), v_own)
    n, nq = q.shape[:2]
    n_sel = sel_slots.shape[0]
    head_ix = jnp.arange(N_HEADS)[:, None]
    blk_ar = jnp.arange(MOBA_BLOCK)

    def slot_logits(args):
        sel, ok = args
        kpos = sel[..., None] * MOBA_BLOCK + blk_ar
        lg = jnp.einsum('nqhd,nqhkd->nqhk', q, rows_k(sel).astype(q.dtype),
                        preferred_element_type=jnp.float32)
        lg = lg + rel_bias[rel_bucket(qpos[None, :, None, None] - kpos), head_ix]
        return jnp.where(ok, lg, -jnp.inf)

    lg_sel = lax.map(slot_logits, (sel_slots, slot_ok))
    lg_sel = jnp.moveaxis(lg_sel, 0, 3).reshape(n, nq, N_HEADS, n_sel * MOBA_BLOCK)
    p = jax.nn.softmax(jnp.concatenate([lg_sel, lg_own], axis=-1), axis=-1)
    p_sel = jnp.moveaxis(p[..., :n_sel * MOBA_BLOCK].reshape(n, nq, N_HEADS, n_sel, MOBA_BLOCK), 3, 0)
    p_own = p[..., n_sel * MOBA_BLOCK:]

    def slot_values(args):
        sel, ps = args
        vs = rows_v(sel)
        return jnp.einsum('nqhk,nqhkd->nqhd', ps.astype(vs.dtype), vs)

    o = lax.map(slot_values, (sel_slots, p_sel)).sum(axis=0)
    return o + jnp.einsum('nqhk,nkhd->nqhd', p_own.astype(v_own.dtype), v_own)


def moba_prompt_prepare(k, v):
    n_seq, seq = k.shape[:2]
    n_blk = -(-seq // MOBA_BLOCK)
    pad = ((0, 0), (0, n_blk * MOBA_BLOCK - seq), (0, 0), (0, 0))
    kp, vp = jnp.pad(k, pad), jnp.pad(v, pad)
    k_blocks = kp.reshape(n_seq, n_blk, MOBA_BLOCK, N_HEADS, HEAD_DIM)
    k_mean = k_blocks.astype(jnp.float32).sum(axis=2) / MOBA_BLOCK
    kb = k_blocks.transpose(0, 3, 1, 2, 4)
    vb = vp.reshape(n_seq, n_blk, MOBA_BLOCK, N_HEADS, HEAD_DIM).transpose(0, 3, 1, 2, 4)
    return (kp, vp, k_mean, kb, vb)


def moba_prompt_attend(q, ctx, rel_bias):
    kp, vp, k_mean, kb, vb = ctx
    n_seq, seq = q.shape[:2]
    n_blk = k_mean.shape[1]
    n_sel = min(MOBA_TOPK, n_blk - 1)
    seq_ix = jnp.arange(n_seq)[:, None, None]
    head_ix = jnp.arange(N_HEADS)[None, None, :]
    rows_k = lambda sel: kb[seq_ix, head_ix, sel]
    rows_v = lambda sel: vb[seq_ix, head_ix, sel]

    def query_block(qb):
        q0 = qb * Q_BLOCK
        qc = lax.dynamic_slice_in_dim(q, q0, Q_BLOCK, axis=1)
        qpos = q0 + jnp.arange(Q_BLOCK)
        cur = q0 // MOBA_BLOCK
        k_own = lax.dynamic_slice_in_dim(kp, cur * MOBA_BLOCK, MOBA_BLOCK, axis=1)
        v_own = lax.dynamic_slice_in_dim(vp, cur * MOBA_BLOCK, MOBA_BLOCK, axis=1)
        lg_own = own_block_logits(qc, qpos, k_own, cur * MOBA_BLOCK + jnp.arange(MOBA_BLOCK), rel_bias)
        if n_sel == 0:
            return attend_selected(qc, qpos, None, None, None, None, lg_own, v_own, rel_bias)
        gate = jnp.einsum('nqhd,nbhd->nqhb', qc.astype(jnp.float32), k_mean)
        gate = jnp.where(jnp.arange(n_blk) < cur, gate, -jnp.inf)
        _, idx = lax.top_k(gate, n_sel)
        slot_ok = jnp.arange(n_sel) < cur
        return attend_selected(qc, qpos, jnp.moveaxis(idx, -1, 0), slot_ok, rows_k, rows_v,
                               lg_own, v_own, rel_bias)

    out = lax.map(query_block, jnp.arange(seq // Q_BLOCK))
    return out.transpose(1, 0, 2, 3, 4).reshape(n_seq, seq, N_HEADS, HEAD_DIM)


def moba_sample_prepare(k_new, v_new, cache_k, cache_v, page_table):
    n_dec = k_new.shape[0]
    ppb = MOBA_BLOCK // PAGE_SIZE
    n_full = (page_table.shape[1] * PAGE_SIZE) // MOBA_BLOCK
    own_pages = page_table[:, n_full * ppb:]
    n_own_past = own_pages.shape[1] * PAGE_SIZE
    k_own = jnp.concatenate([cache_k[own_pages].reshape(n_dec, n_own_past, N_HEADS, HEAD_DIM).astype(k_new.dtype), k_new], axis=1)
    v_own = jnp.concatenate([cache_v[own_pages].reshape(n_dec, n_own_past, N_HEADS, HEAD_DIM).astype(v_new.dtype), v_new], axis=1)
    page_sum = cache_k.astype(jnp.float32).sum(axis=1)
    k_mean = page_sum[page_table[:, :n_full * ppb]].reshape(n_dec, n_full, ppb, N_HEADS, HEAD_DIM).sum(axis=2) / MOBA_BLOCK
    return (k_own, v_own, k_mean)


def moba_sample_attend(q, ctx, cache_k, cache_v, page_table, rel_bias):
    k_own, v_own, k_mean = ctx
    n_dec, n_new = q.shape[:2]
    past = page_table.shape[1] * PAGE_SIZE
    n_full = k_mean.shape[1]
    qpos = past + jnp.arange(n_new)
    lg_own = own_block_logits(q, qpos, k_own, n_full * MOBA_BLOCK + jnp.arange(k_own.shape[1]), rel_bias)
    n_sel = min(MOBA_TOPK, n_full)
    if n_sel == 0:
        return attend_selected(q, qpos, None, None, None, None, lg_own, v_own, rel_bias)
    gate = jnp.einsum('nqhd,nbhd->nqhb', q.astype(jnp.float32), k_mean)
    _, idx = lax.top_k(gate, n_sel)
    ppb = MOBA_BLOCK // PAGE_SIZE
    seq_ix = jnp.arange(n_dec)[:, None, None, None]
    row_ix = jnp.arange(PAGE_SIZE)
    head_ix = jnp.arange(N_HEADS)[:, None, None]

    def rows(cache, sel):
        phys = page_table[seq_ix, sel[..., None] * ppb + jnp.arange(ppb)]
        return cache[phys[..., None], row_ix, head_ix].reshape(sel.shape + (MOBA_BLOCK, HEAD_DIM))

    return attend_selected(q, qpos, jnp.moveaxis(idx, -1, 0), jnp.ones((n_sel,), bool),
                           lambda s: rows(cache_k, s), lambda s: rows(cache_v, s),
                           lg_own, v_own, rel_bias)


def setup_inputs(seed: int = 0) -> dict:
    key = jax.random.key(seed)
    ks = jax.random.split(key, 40)
    f32 = jnp.float32
    D = D_MODEL
    n_pages = PAST_LEN // PAGE_SIZE
    n_used = DEC_BATCH * n_pages
    n_phys = n_used + max(1, n_used // 4)

    def nrm(i, shape, scale):
        return jax.random.normal(ks[i], shape, f32) * scale

    page_table = jax.random.permutation(ks[0], n_phys)[:n_used].reshape(DEC_BATCH, n_pages).astype(jnp.int32)
    return {
        'x_prompt': nrm(1, (BATCH, SEQ, D), 1.0),
        'x_sample': nrm(2, (DEC_BATCH, DEC_SEQ, D), 1.0),
        'c_prompt': nrm(3, (BATCH, D), 1.0),
        'c_sample': nrm(4, (DEC_BATCH, D), 1.0),
        'cache_k': nrm(5, (n_phys, PAGE_SIZE, N_HEADS, HEAD_DIM), 1.0),
        'cache_v': nrm(6, (n_phys, PAGE_SIZE, N_HEADS, HEAD_DIM), 1.0),
        'state_conv': nrm(7, (N_A_LAYERS, DEC_BATCH, CONV_W - 1, D), 0.5),
        'page_table': page_table,
        'ada_w': nrm(8, (DEPTH, D, 6 * D), 0.2 * D ** -0.5),
        'ada_b': nrm(9, (DEPTH, 6 * D), 0.02),
        'norm_mix_g': 1.0 + nrm(10, (DEPTH, D), 0.02),
        'norm_ffn_g': 1.0 + nrm(11, (DEPTH, D), 0.02),
        'conv_w_in': nrm(12, (N_A_LAYERS, D, 2 * D), D ** -0.5),
        'conv_w_dw': nrm(13, (N_A_LAYERS, CONV_W, D), CONV_W ** -0.5),
        'conv_b_dw': nrm(14, (N_A_LAYERS, D), 0.02),
        'conv_ln_g': 1.0 + nrm(15, (N_A_LAYERS, D), 0.02),
        'conv_ln_b': nrm(16, (N_A_LAYERS, D), 0.02),
        'conv_w_out': nrm(17, (N_A_LAYERS, D, D), D ** -0.5),
        'kv_ada_w': nrm(18, (D, 2 * D), 0.2 * D ** -0.5),
        'kv_ada_b': nrm(19, (2 * D,), 0.02),
        'norm_kv_g': 1.0 + nrm(20, (D,), 0.02),
        'kv_w_k': nrm(21, (D, D), D ** -0.5),
        'kv_w_v': nrm(22, (D, D), D ** -0.5),
        'kv_g_k': 1.0 + nrm(23, (HEAD_DIM,), 0.02),
        'attn_w_q': nrm(24, (N_B_LAYERS, D, D), D ** -0.5),
        'attn_g_q': 1.0 + nrm(25, (N_B_LAYERS, HEAD_DIM), 0.02),
        'attn_w_o': nrm(26, (N_B_LAYERS, D, D), D ** -0.5),
        'rel_bias': nrm(27, (REL_BUCKETS, N_HEADS), 0.5),
        'ffn_w_gate': nrm(28, (DEPTH, D, D_FF), D ** -0.5),
        'ffn_w_up': nrm(29, (DEPTH, D, D_FF), D ** -0.5),
        'ffn_w_down': nrm(30, (DEPTH, D_FF, D), D_FF ** -0.5),
    }


def reference(x_prompt, x_sample, c_prompt, c_sample, cache_k, cache_v, state_conv, page_table,
              ada_w, ada_b, norm_mix_g, norm_ffn_g,
              conv_w_in, conv_w_dw, conv_b_dw, conv_ln_g, conv_ln_b, conv_w_out,
              kv_ada_w, kv_ada_b, norm_kv_g, kv_w_k, kv_w_v, kv_g_k,
              attn_w_q, attn_g_q, attn_w_o, rel_bias,
              ffn_w_gate, ffn_w_up, ffn_w_down):

    def run(x, c, conv_bufs, prepare, attend):
        new_bufs = []
        k = v = ctx = None
        for layer in range(DEPTH):
            sh_m, sc_m, g_m, sh_f, sc_f, g_f = jnp.split(c @ ada_w[layer] + ada_b[layer], 6, axis=-1)
            if layer == N_A_LAYERS:
                sh_kv, sc_kv = jnp.split(c @ kv_ada_w + kv_ada_b, 2, axis=-1)
                h_kv = modulate(rms_norm(x, norm_kv_g), sh_kv, sc_kv)
                k = rms_norm(split_heads(h_kv @ kv_w_k), kv_g_k)
                v = split_heads(h_kv @ kv_w_v)
                ctx = prepare(k, v)
            h = modulate(rms_norm(x, norm_mix_g[layer]), sh_m, sc_m)
            if layer < N_A_LAYERS:
                y, buf = conv_module(h, conv_bufs[layer], conv_w_in[layer], conv_w_dw[layer], conv_b_dw[layer],
                                     conv_ln_g[layer], conv_ln_b[layer], conv_w_out[layer])
                new_bufs.append(buf)
            else:
                j = layer - N_A_LAYERS
                q = rms_norm(split_heads(h @ attn_w_q[j]), attn_g_q[j]) * HEAD_DIM ** -0.5
                o = attend(q, ctx)
                y = o.reshape(o.shape[:2] + (D_MODEL,)) @ attn_w_o[j]
            x = x + g_m[:, None, :] * y
            h = modulate(rms_norm(x, norm_ffn_g[layer]), sh_f, sc_f)
            x = x + g_f[:, None, :] * swiglu(h, ffn_w_gate[layer], ffn_w_up[layer], ffn_w_down[layer])
        return x, k, v, jnp.stack(new_bufs)

    zero_bufs = jnp.zeros((N_A_LAYERS, x_prompt.shape[0], CONV_W - 1, D_MODEL), x_prompt.dtype)
    y_prompt, k_prompt, v_prompt, conv_prompt = run(
        x_prompt, c_prompt, zero_bufs,
        moba_prompt_prepare,
        lambda q, ctx: moba_prompt_attend(q, ctx, rel_bias))
    y_sample, k_sample, v_sample, conv_sample = run(
        x_sample, c_sample, state_conv,
        lambda k, v: moba_sample_prepare(k, v, cache_k, cache_v, page_table),
        lambda q, ctx: moba_sample_attend(q, ctx, cache_k, cache_v, page_table, rel_bias))
    return (y_prompt, y_sample, k_prompt, v_prompt, k_sample, v_sample, conv_prompt, conv_sample)
```

```python
import functools
import math

import jax
import jax.numpy as jnp
from jax import lax
from jax.experimental import pallas as pl
from jax.experimental.pallas import tpu as pltpu

F32 = jnp.float32
BF16 = jnp.bfloat16

D_MODEL = 2048
HEAD_DIM = 128
N_HEADS = D_MODEL // HEAD_DIM
CONV_W = 31
MOBA_BLOCK = 256
MOBA_TOPK = 3
REL_BUCKETS = 32
REL_MAX_DIST = 128
PAGE_SIZE = 128
NORM_EPS = 1e-6

LANES = 128
CONV_HALO = 32
VMEM_LIMIT_BYTES = 56 * 1024 * 1024

_CONTRACT_LAST = (((1,), (1,)), ((), ()))


def _params(n_axes):
    return pltpu.CompilerParams(dimension_semantics=("arbitrary",) * n_axes,
                                vmem_limit_bytes=VMEM_LIMIT_BYTES)


def _sigmoid(x):
    return 1.0 / (1.0 + jnp.exp(-x))


def _head_rms(a, gain, scale):
    outs = []
    for c in range(a.shape[1] // HEAD_DIM):
        blk = a[:, c * HEAD_DIM:(c + 1) * HEAD_DIM]
        ms = jnp.mean(blk * blk, axis=-1, keepdims=True)
        y = blk * lax.rsqrt(ms + NORM_EPS) * gain
        if scale != 1.0:
            y = y * scale
        outs.append(y)
    return jnp.concatenate(outs, axis=1)


def _fused_mm_body(*refs, prologue, epilogue, n_w, n_out, head_scale, row_chunk):
    refs = list(refs)
    h_ref = refs.pop() if prologue is not None else None
    outs = refs[len(refs) - n_out:]
    it = iter(refs[:len(refs) - n_out])
    x_ref = next(it)
    if prologue == "rmsmod":
        g_ref, sh_ref, sc_ref = next(it), next(it), next(it)
    elif prologue == "lnsilu":
        cb_ref, lg_ref, lb_ref = next(it), next(it), next(it)
    w_refs = [next(it) for _ in range(n_w)]
    if epilogue == "bias":
        b_ref = next(it)
    elif epilogue == "resid":
        res_ref, gate_ref = next(it), next(it)
    elif epilogue in ("headnorm", "kv"):
        hg_ref = next(it)

    bm = x_ref.shape[0]

    if prologue is not None:
        @pl.when(pl.program_id(1) == 0)
        def _prologue():
            def vec(ref, r0):
                return ref[...] if ref.shape[0] == 1 else ref[pl.ds(r0, row_chunk), :]

            def chunk(c, carry):
                r0 = pl.multiple_of(c * row_chunk, row_chunk)
                x = x_ref[pl.ds(r0, row_chunk), :].astype(F32)
                if prologue == "rmsmod":
                    ms = jnp.mean(x * x, axis=-1, keepdims=True)
                    y = x * lax.rsqrt(ms + NORM_EPS) * g_ref[...]
                    h = y * (1.0 + vec(sc_ref, r0)) + vec(sh_ref, r0)
                else:
                    x = x + cb_ref[...]
                    mu = jnp.mean(x, axis=-1, keepdims=True)
                    xc = x - mu
                    y = xc * lax.rsqrt(jnp.mean(xc * xc, axis=-1, keepdims=True) + NORM_EPS)
                    y = y * lg_ref[...] + lb_ref[...]
                    h = y * _sigmoid(y)
                h_ref[pl.ds(r0, row_chunk), :] = h.astype(BF16)
                return carry

            lax.fori_loop(0, bm // row_chunk, chunk, 0)

        h = h_ref[...]
    else:
        h = x_ref[...].astype(BF16)

    accs = [jnp.dot(h, w[...].astype(BF16), preferred_element_type=F32) for w in w_refs]

    if epilogue == "bias":
        res = [accs[0] + b_ref[...]]
    elif epilogue == "glu":
        res = [accs[0] * _sigmoid(accs[1])]
    elif epilogue == "swiglu":
        res = [accs[0] * _sigmoid(accs[0]) * accs[1]]
    elif epilogue == "resid":
        res = [res_ref[...] + gate_ref[...] * accs[0]]
    elif epilogue == "headnorm":
        res = [_head_rms(accs[0], hg_ref[...], head_scale)]
    elif epilogue == "kv":
        k = _head_rms(accs[0], hg_ref[...], head_scale)
        res = [k, k, accs[1], accs[1]]
    else:
        res = [accs[0]]
    if len(res) == 1:
        res = res * n_out
    for o_ref, r in zip(outs, res):
        o_ref[...] = r.astype(o_ref.dtype)


class _Mod:
    def __init__(self, arr, chunk, mode, rows_per_seq):
        self.arr, self.chunk, self.mode, self.rows_per_seq = arr, chunk, mode, rows_per_seq

    def spec(self, bm, width, follow_n):
        per = D_MODEL // width
        base = self.chunk * per
        if self.mode == "seq":
            blocks_per_seq = self.rows_per_seq // bm
            return pl.BlockSpec(
                (None, 1, width),
                lambda i, j: (i // blocks_per_seq, 0, base + (j if follow_n else 0)))
        return pl.BlockSpec((bm, width), lambda i, j: (i, base + (j if follow_n else 0)))


def _fused_mm(x, ws, *, bm, bn, n_cols, out_dtypes, prologue=None, pro_args=(), epilogue=None,
              epi_args=(), head_scale=1.0, name):
    m, k = x.shape
    assert m % bm == 0 and n_cols % bn == 0
    grid = (m // bm, n_cols // bn)
    row_chunk = 128 if bm % 128 == 0 else bm

    in_specs = [pl.BlockSpec((bm, k), lambda i, j: (i, 0))]
    args = [x]
    full_vec = pl.BlockSpec((1, k), lambda i, j: (0, 0))
    if prologue == "rmsmod":
        g, sh, sc = pro_args
        in_specs += [full_vec, sh.spec(bm, k, False), sc.spec(bm, k, False)]
        args += [g.reshape(1, k), sh.arr, sc.arr]
    elif prologue == "lnsilu":
        in_specs += [full_vec] * 3
        args += [a.reshape(1, k) for a in pro_args]
    for w, col0 in ws:
        assert col0 % bn == 0
        off = col0 // bn
        in_specs.append(pl.BlockSpec((k, bn), lambda i, j, off=off: (0, off + j)))
        args.append(w)
    if epilogue == "bias":
        in_specs.append(pl.BlockSpec((1, bn), lambda i, j: (0, j)))
        args.append(epi_args[0].reshape(1, n_cols))
    elif epilogue == "resid":
        res, gate = epi_args
        in_specs += [pl.BlockSpec((bm, bn), lambda i, j: (i, j)), gate.spec(bm, bn, True)]
        args += [res, gate.arr]
    elif epilogue in ("headnorm", "kv"):
        in_specs.append(pl.BlockSpec((1, HEAD_DIM), lambda i, j: (0, 0)))
        args.append(epi_args[0].reshape(1, HEAD_DIM))

    out_shape = [jax.ShapeDtypeStruct((m, n_cols), dt) for dt in out_dtypes]
    out_specs = [pl.BlockSpec((bm, bn), lambda i, j: (i, j)) for _ in out_dtypes]
    scratch = [pltpu.VMEM((bm, k), BF16)] if prologue is not None else []
    body = functools.partial(_fused_mm_body, prologue=prologue, epilogue=epilogue, n_w=len(ws),
                             n_out=len(out_dtypes), head_scale=head_scale, row_chunk=row_chunk)
    out = pl.pallas_call(
        body, grid=grid, in_specs=in_specs, out_specs=out_specs, out_shape=out_shape,
        scratch_shapes=scratch, compiler_params=_params(2), name=name)(*args)
    return out[0] if len(out) == 1 else out


def _conv_prompt_body(z_ref, halo_ref, w_ref, o_ref, ext_ref, *, sub_rows):
    bt = z_ref.shape[0]
    tb = pl.program_id(1)

    @pl.when(tb == 0)
    def _():
        ext_ref[0:CONV_HALO, :] = jnp.zeros((CONV_HALO, ext_ref.shape[1]), F32)

    @pl.when(tb > 0)
    def _():
        ext_ref[0:CONV_HALO, :] = halo_ref[...]

    ext_ref[CONV_HALO:, :] = z_ref[...]
    first = CONV_HALO - (CONV_W - 1)
    for r0 in range(0, bt, sub_rows):
        acc = w_ref[0:1, :] * ext_ref[first + r0:first + r0 + sub_rows, :]
        for j in range(1, CONV_W):
            acc = acc + w_ref[j:j + 1, :] * ext_ref[first + j + r0:first + j + r0 + sub_rows, :]
        o_ref[r0:r0 + sub_rows, :] = acc


def _conv_prompt(z, w_dw, n_seq, seq):
    bt, bd = 512, 256
    t_blocks = seq // bt
    halo_per_seq = seq // CONV_HALO
    halo_per_blk = bt // CONV_HALO
    grid = (n_seq, t_blocks, D_MODEL // bd)
    return pl.pallas_call(
        functools.partial(_conv_prompt_body, sub_rows=32),
        grid=grid,
        in_specs=[
            pl.BlockSpec((bt, bd), lambda n, t, d: (n * t_blocks + t, d)),
            pl.BlockSpec((CONV_HALO, bd),
                         lambda n, t, d: (jnp.maximum(n * halo_per_seq + t * halo_per_blk - 1, 0), d)),
            pl.BlockSpec((CONV_W, bd), lambda n, t, d: (0, d)),
        ],
        out_specs=pl.BlockSpec((bt, bd), lambda n, t, d: (n * t_blocks + t, d)),
        out_shape=jax.ShapeDtypeStruct(z.shape, F32),
        scratch_shapes=[pltpu.VMEM((CONV_HALO + bt, bd), F32)],
        compiler_params=_params(3), name="conv_prompt")(z, z, w_dw)


def _conv_sample_body(st_ref, z_ref, w_ref, o_ref):
    n_new = z_ref.shape[0]
    n_old = st_ref.shape[0]
    for t in range(n_new):
        acc = None
        for i in range(t, n_old):
            term = w_ref[i - t:i - t + 1, :] * st_ref[i]
            acc = term if acc is None else acc + term
        for s in range(t + 1):
            jw = n_old + s - t
            acc = acc + w_ref[jw:jw + 1, :] * z_ref[s]
        o_ref[t] = acc


def _conv_sample(st_t, z_t, w_dw):
    n_old, n_seq, _ = st_t.shape
    n_new = z_t.shape[0]
    bd = 512
    return pl.pallas_call(
        _conv_sample_body,
        grid=(D_MODEL // bd,),
        in_specs=[
            pl.BlockSpec((n_old, n_seq, bd), lambda d: (0, 0, d)),
            pl.BlockSpec((n_new, n_seq, bd), lambda d: (0, 0, d)),
            pl.BlockSpec((CONV_W, bd), lambda d: (0, d)),
        ],
        out_specs=pl.BlockSpec((n_new, n_seq, bd), lambda d: (0, 0, d)),
        out_shape=jax.ShapeDtypeStruct(z_t.shape, F32),
        compiler_params=_params(1), name="conv_sample")(st_t, z_t, w_dw)


def _rel_bucket(dist):
    n = jnp.maximum(dist, 0)
    max_exact = REL_BUCKETS // 2
    nf = jnp.maximum(n, 1).astype(F32)
    large = max_exact + (jnp.log(nf / max_exact) / math.log(REL_MAX_DIST / max_exact)
                         * (REL_BUCKETS - max_exact)).astype(jnp.int32)
    large = jnp.minimum(large, REL_BUCKETS - 1)
    return jnp.where(n < max_exact, n, large)


def _top_blocks(gate, lane, topk):
    cnt = jnp.zeros(gate.shape, jnp.int32)
    for b in range(gate.shape[1]):
        col = gate[:, b:b + 1]
        beats = (col > gate) | ((col == gate) & (b < lane))
        cnt = cnt + beats.astype(jnp.int32)
    return cnt < topk


def _kmean_body(k_ref, o_ref):
    o_ref[...] = jnp.sum(k_ref[...], axis=0, keepdims=True) / MOBA_BLOCK


def _block_means(k, n_blocks):
    return pl.pallas_call(
        _kmean_body, grid=(n_blocks,),
        in_specs=[pl.BlockSpec((MOBA_BLOCK, D_MODEL), lambda b: (b, 0))],
        out_specs=pl.BlockSpec((None, 1, D_MODEL), lambda b: (b, 0, 0)),
        out_shape=jax.ShapeDtypeStruct((n_blocks, 1, D_MODEL), F32),
        compiler_params=_params(1), name="kmean_prompt")(k)


def _attn_prompt_body(q_ref, k_ref, v_ref, km_ref, bias_ref, o_ref):
    blk = MOBA_BLOCK
    n_blk = km_ref.shape[0]
    qb = pl.program_id(2)
    q = q_ref[...]
    gate = lax.dot_general(q, km_ref[...], _CONTRACT_LAST, precision=lax.Precision.HIGHEST,
                           preferred_element_type=F32)
    lane = lax.broadcasted_iota(jnp.int32, (blk, n_blk), 1)
    valid = lane < qb
    gate = jnp.where(valid, gate, -jnp.inf)
    sel = (valid & _top_blocks(gate, lane, MOBA_TOPK)).astype(F32)

    qh = q.astype(BF16)
    r_own = pl.multiple_of(qb * blk, blk)
    s = lax.dot_general(qh, k_ref[pl.ds(r_own, blk), :], _CONTRACT_LAST,
                        preferred_element_type=F32) + bias_ref[0]
    m = jnp.max(s, axis=-1, keepdims=True)
    p = jnp.exp(s - m)
    l = jnp.sum(p, axis=-1, keepdims=True)
    acc = jnp.dot(p.astype(BF16), v_ref[pl.ds(r_own, blk), :], preferred_element_type=F32)

    def body(b, carry):
        m, l, acc = carry
        r0 = pl.multiple_of(b * blk, blk)
        s = lax.dot_general(qh, k_ref[pl.ds(r0, blk), :], _CONTRACT_LAST,
                            preferred_element_type=F32) + bias_ref[jnp.minimum(qb - b, 2)]
        sel_b = jnp.sum(jnp.where(lane == b, sel, 0.0), axis=1, keepdims=True) > 0.0
        s = jnp.where(sel_b, s, -jnp.inf)
        m_new = jnp.maximum(m, jnp.max(s, axis=-1, keepdims=True))
        alpha = jnp.exp(m - m_new)
        p = jnp.exp(s - m_new)
        l = alpha * l + jnp.sum(p, axis=-1, keepdims=True)
        acc = alpha * acc + jnp.dot(p.astype(BF16), v_ref[pl.ds(r0, blk), :],
                                    preferred_element_type=F32)
        return m_new, l, acc

    m, l, acc = lax.fori_loop(0, qb, body, (m, l, acc))
    o_ref[...] = (acc / l).astype(o_ref.dtype)


def _attn_prompt(q, k_bf, v_bf, k_mean, bias_tiles, n_seq, seq):
    n_blk = seq // MOBA_BLOCK
    grid = (n_seq, N_HEADS, n_blk)
    return pl.pallas_call(
        _attn_prompt_body, grid=grid,
        in_specs=[
            pl.BlockSpec((MOBA_BLOCK, HEAD_DIM), lambda n, h, b: (n * n_blk + b, h)),
            pl.BlockSpec((seq, HEAD_DIM), lambda n, h, b: (n, h)),
            pl.BlockSpec((seq, HEAD_DIM), lambda n, h, b: (n, h)),
            pl.BlockSpec((None, n_blk, HEAD_DIM), lambda n, h, b: (n, 0, h)),
            pl.BlockSpec((None, 3, MOBA_BLOCK, MOBA_BLOCK), lambda n, h, b: (h, 0, 0, 0)),
        ],
        out_specs=pl.BlockSpec((MOBA_BLOCK, HEAD_DIM), lambda n, h, b: (n * n_blk + b, h)),
        out_shape=jax.ShapeDtypeStruct(q.shape, BF16),
        compiler_params=_params(3), name="attn_prompt")(q, k_bf, v_bf, k_mean, bias_tiles)


def _attn_sample_body(pt_ref, q_ref, kn_ref, vn_ref, ck_ref, cv_ref, bp_ref, bo_ref, o_ref,
                      qf_ref, qh_ref, kblk_ref, m_ref, l_ref, acc_ref):
    del pt_ref
    n_new = q_ref.shape[0]
    rows = n_new * N_HEADS
    n_full = kblk_ref.shape[0]
    pages_per_blk = MOBA_BLOCK // PAGE_SIZE
    j = pl.program_id(1)
    last = pl.num_programs(1) - 1
    row = lax.broadcasted_iota(jnp.int32, (rows, D_MODEL), 0)
    col = lax.broadcasted_iota(jnp.int32, (rows, D_MODEL), 1)
    head_mask = lax.shift_right_logical(col, HEAD_DIM.bit_length() - 1) == (row & (N_HEADS - 1))

    @pl.when(j == 0)
    def _init():
        q = q_ref[...]
        q_rep = jnp.broadcast_to(q[:, None, :], (n_new, N_HEADS, D_MODEL)).reshape(rows, D_MODEL)
        q_bd = jnp.where(head_mask, q_rep, 0.0)
        qf_ref[...] = q_bd
        qh_ref[...] = q_bd.astype(BF16)
        kblk_ref[...] = jnp.zeros(kblk_ref.shape, F32)

    b = j // pages_per_blk
    kp = ck_ref[...]
    kblk_ref[pl.ds(b, 1), :] += jnp.sum(kp, axis=0, keepdims=True)
    s = lax.dot_general(qh_ref[...], kp.astype(BF16), _CONTRACT_LAST,
                        preferred_element_type=F32) + bp_ref[j]
    mp = jnp.max(s, axis=-1, keepdims=True)
    vp = cv_ref[...].astype(BF16)

    @pl.when(j % pages_per_blk == 0)
    def _first_page():
        p = jnp.exp(s - mp)
        m_ref[b] = mp
        l_ref[b] = jnp.sum(p, axis=-1, keepdims=True)
        acc_ref[b] = jnp.dot(p.astype(BF16), vp, preferred_element_type=F32)

    @pl.when(j % pages_per_blk != 0)
    def _next_page():
        m_old = m_ref[b]
        m_new = jnp.maximum(m_old, mp)
        alpha = jnp.exp(m_old - m_new)
        p = jnp.exp(s - m_new)
        m_ref[b] = m_new
        l_ref[b] = alpha * l_ref[b] + jnp.sum(p, axis=-1, keepdims=True)
        acc_ref[b] = alpha * acc_ref[b] + jnp.dot(p.astype(BF16), vp, preferred_element_type=F32)

    @pl.when(j == last)
    def _combine():
        qf = qf_ref[...]
        k_mean = kblk_ref[...] / MOBA_BLOCK
        gate = lax.dot_general(qf, k_mean, _CONTRACT_LAST, precision=lax.Precision.HIGHEST,
                               preferred_element_type=F32)
        lane = lax.broadcasted_iota(jnp.int32, (rows, n_full), 1)
        sel = _top_blocks(gate, lane, MOBA_TOPK).astype(F32)
        sel_cols = [sel[:, c:c + 1] > 0.0 for c in range(n_full)]

        kn = kn_ref[...]
        vn = vn_ref[...]
        s_own = [jnp.sum(qf * kn[t:t + 1, :], axis=-1, keepdims=True) + bo_ref[:, t:t + 1]
                 for t in range(n_new)]
        m = s_own[0]
        for t in range(1, n_new):
            m = jnp.maximum(m, s_own[t])
        for c in range(n_full):
            m = jnp.maximum(m, jnp.where(sel_cols[c], m_ref[c], -jnp.inf))
        l = jnp.zeros_like(m)
        acc = jnp.zeros((rows, D_MODEL), F32)
        for t in range(n_new):
            p = jnp.exp(s_own[t] - m)
            l = l + p
            acc = acc + p * vn[t:t + 1, :]
        for c in range(n_full):
            w = jnp.where(sel_cols[c], jnp.exp(m_ref[c] - m), 0.0)
            l = l + w * l_ref[c]
            acc = acc + w * acc_ref[c]
        o = jnp.where(head_mask, acc / l, 0.0)
        o_ref[...] = jnp.sum(o.reshape(n_new, N_HEADS, D_MODEL), axis=1)


def _attn_sample(q, k_new, v_new, cache_k, cache_v, page_table, bias_past, bias_own):
    n_dec, n_new, _ = q.shape
    n_pages = page_table.shape[1]
    n_full = n_pages * PAGE_SIZE // MOBA_BLOCK
    rows = n_new * N_HEADS
    seq_spec = pl.BlockSpec((None, n_new, D_MODEL), lambda n, j, pt: (n, 0, 0))
    page_spec = pl.BlockSpec((None, PAGE_SIZE, D_MODEL), lambda n, j, pt: (pt[n, j], 0, 0))
    grid_spec = pltpu.PrefetchScalarGridSpec(
        num_scalar_prefetch=1,
        grid=(n_dec, n_pages),
        in_specs=[
            seq_spec, seq_spec, seq_spec, page_spec, page_spec,
            pl.BlockSpec((n_pages, rows, PAGE_SIZE), lambda n, j, pt: (0, 0, 0)),
            pl.BlockSpec((rows, n_new), lambda n, j, pt: (0, 0)),
        ],
        out_specs=seq_spec,
        scratch_shapes=[
            pltpu.VMEM((rows, D_MODEL), F32),
            pltpu.VMEM((rows, D_MODEL), BF16),
            pltpu.VMEM((n_full, D_MODEL), F32),
            pltpu.VMEM((n_full, rows, 1), F32),
            pltpu.VMEM((n_full, rows, 1), F32),
            pltpu.VMEM((n_full, rows, D_MODEL), F32),
        ])
    n_phys = cache_k.shape[0]
    return pl.pallas_call(
        _attn_sample_body, grid_spec=grid_spec,
        out_shape=jax.ShapeDtypeStruct(q.shape, F32),
        compiler_params=_params(2), name="attn_sample")(
            page_table, q, k_new, v_new,
            cache_k.reshape(n_phys, PAGE_SIZE, D_MODEL), cache_v.reshape(n_phys, PAGE_SIZE, D_MODEL),
            bias_past, bias_own)


def _prompt_bias_tiles(rel_bias):
    assert REL_MAX_DIST <= MOBA_BLOCK
    i = jnp.arange(MOBA_BLOCK)[:, None]
    j = jnp.arange(MOBA_BLOCK)[None, :]
    tiles = []
    for delta in range(3):
        dist = delta * MOBA_BLOCK + i - j
        t = rel_bias[_rel_bucket(dist)]
        if delta == 0:
            t = jnp.where((dist >= 0)[..., None], t, -jnp.inf)
        tiles.append(t)
    return jnp.stack(tiles, axis=0).transpose(3, 0, 1, 2)


def _sample_bias_tables(rel_bias, past, n_new):
    qpos = past + jnp.arange(n_new)
    dist = qpos[:, None] - jnp.arange(past)[None, :]
    bp = rel_bias[_rel_bucket(dist)].transpose(0, 2, 1)
    bp = bp.reshape(n_new * N_HEADS, past // PAGE_SIZE, PAGE_SIZE).transpose(1, 0, 2)
    d_own = jnp.arange(n_new)[:, None] - jnp.arange(n_new)[None, :]
    bo = jnp.where((d_own >= 0)[..., None], rel_bias[_rel_bucket(d_own)], -jnp.inf)
    bo = bo.transpose(0, 2, 1).reshape(n_new * N_HEADS, n_new)
    return bp, bo


def _trunk(x, mods, wts, *, mode, rows_per_seq, bm, conv_fn, attn_fn):
    mod0, mod1, modkv = mods
    M = lambda arr, c: _Mod(arr, c, mode, rows_per_seq)
    d_ff = wts["ffn_gate"][0].shape[1]
    bn = 512

    def ffn(xin, layer, mod):
        gu = _fused_mm(xin, [(wts["ffn_gate"][layer], 0), (wts["ffn_up"][layer], 0)], bm=bm, bn=bn,
                       n_cols=d_ff, out_dtypes=[BF16], prologue="rmsmod",
                       pro_args=(wts["norm_ffn_g"][layer], M(mod, 3), M(mod, 4)),
                       epilogue="swiglu", name=f"ffn_up{layer}_{mode}")
        return _fused_mm(gu, [(wts["ffn_down"][layer], 0)], bm=min(bm, 512), bn=bn, n_cols=D_MODEL,
                         out_dtypes=[F32], epilogue="resid", epi_args=(xin, M(mod, 5)),
                         name=f"ffn_down{layer}_{mode}")

    z = _fused_mm(x, [(wts["conv_in"], 0), (wts["conv_in"], D_MODEL)], bm=bm, bn=bn, n_cols=D_MODEL,
                  out_dtypes=[F32], prologue="rmsmod",
                  pro_args=(wts["norm_mix_g"][0], M(mod0, 0), M(mod0, 1)), epilogue="glu",
                  name=f"conv_in_{mode}")
    yc = conv_fn(z)
    x = _fused_mm(yc, [(wts["conv_out"], 0)], bm=bm, bn=bn, n_cols=D_MODEL, out_dtypes=[F32],
                  prologue="lnsilu", pro_args=(wts["conv_b_dw"], wts["conv_ln_g"], wts["conv_ln_b"]),
                  epilogue="resid", epi_args=(x, M(mod0, 2)), name=f"conv_out_{mode}")
    x = ffn(x, 0, mod0)

    k, k_bf, v, v_bf = _fused_mm(
        x, [(wts["kv_k"], 0), (wts["kv_v"], 0)], bm=bm, bn=bn, n_cols=D_MODEL,
        out_dtypes=[F32, BF16, F32, BF16], prologue="rmsmod",
        pro_args=(wts["norm_kv_g"], M(modkv, 0), M(modkv, 1)), epilogue="kv",
        epi_args=(wts["kv_g_k"],), name=f"kv_{mode}")

    q = _fused_mm(x, [(wts["attn_q"], 0)], bm=bm, bn=bn, n_cols=D_MODEL, out_dtypes=[F32],
                  prologue="rmsmod", pro_args=(wts["norm_mix_g"][1], M(mod1, 0), M(mod1, 1)),
                  epilogue="headnorm", epi_args=(wts["attn_g_q"],), head_scale=HEAD_DIM ** -0.5,
                  name=f"q_{mode}")
    o = attn_fn(q, k, v, k_bf, v_bf)
    x = _fused_mm(o, [(wts["attn_o"], 0)], bm=bm, bn=bn, n_cols=D_MODEL, out_dtypes=[F32],
                  epilogue="resid", epi_args=(x, M(mod1, 2)), name=f"attn_o_{mode}")
    x = ffn(x, 1, mod1)
    return x, k, v, z


def kernel(x_prompt, x_sample, c_prompt, c_sample, cache_k, cache_v, state_conv, page_table, ada_w, ada_b, norm_mix_g, norm_ffn_g, conv_w_in, conv_w_dw, conv_b_dw, conv_ln_g, conv_ln_b, conv_w_out, kv_ada_w, kv_ada_b, norm_kv_g, kv_w_k, kv_w_v, kv_g_k, attn_w_q, attn_g_q, attn_w_o, rel_bias, ffn_w_gate, ffn_w_up, ffn_w_down):
    n_seq, seq, d = x_prompt.shape
    n_dec, n_new, _ = x_sample.shape
    assert d == D_MODEL and ada_w.shape[0] == 2 and state_conv.shape[0] == 1
    past = page_table.shape[1] * PAGE_SIZE
    assert past % MOBA_BLOCK == 0 and seq % MOBA_BLOCK == 0

    n_c = n_seq + n_dec
    c_rows = -(-n_c // 16) * 16
    c_all = jnp.concatenate([c_prompt, c_sample, jnp.zeros((c_rows - n_c, d), F32)], axis=0)

    def ada(w, b, name):
        return _fused_mm(c_all, [(w, 0)], bm=c_rows, bn=512, n_cols=w.shape[1], out_dtypes=[F32],
                         epilogue="bias", epi_args=(b,), name=name)

    mods = [ada(ada_w[0], ada_b[0], "ada0"), ada(ada_w[1], ada_b[1], "ada1"),
            ada(kv_ada_w, kv_ada_b, "ada_kv")]
    mods_p = [m[:n_seq].reshape(n_seq, 1, -1) for m in mods]
    mods_s = [jnp.repeat(m[n_seq:n_c], n_new, axis=0) for m in mods]

    wts = {
        "norm_mix_g": norm_mix_g, "norm_ffn_g": norm_ffn_g,
        "conv_in": conv_w_in[0].astype(BF16), "conv_out": conv_w_out[0].astype(BF16),
        "conv_b_dw": conv_b_dw[0], "conv_ln_g": conv_ln_g[0], "conv_ln_b": conv_ln_b[0],
        "norm_kv_g": norm_kv_g, "kv_k": kv_w_k.astype(BF16), "kv_v": kv_w_v.astype(BF16),
        "kv_g_k": kv_g_k, "attn_q": attn_w_q[0].astype(BF16), "attn_g_q": attn_g_q[0],
        "attn_o": attn_w_o[0].astype(BF16),
        "ffn_gate": ffn_w_gate.astype(BF16), "ffn_up": ffn_w_up.astype(BF16),
        "ffn_down": ffn_w_down.astype(BF16),
    }
    w_dw = conv_w_dw[0]

    bias_tiles = _prompt_bias_tiles(rel_bias)

    def attn_prompt(q, k, v, k_bf, v_bf):
        k_mean = _block_means(k, n_seq * seq // MOBA_BLOCK).reshape(n_seq, seq // MOBA_BLOCK, D_MODEL)
        return _attn_prompt(q, k_bf, v_bf, k_mean, bias_tiles, n_seq, seq)

    y_p, k_p, v_p, z_p = _trunk(
        x_prompt.reshape(n_seq * seq, d), mods_p, wts, mode="seq", rows_per_seq=seq, bm=1024,
        conv_fn=lambda z: _conv_prompt(z, w_dw, n_seq, seq), attn_fn=attn_prompt)

    bias_past, bias_own = _sample_bias_tables(rel_bias, past, n_new)
    st_t = state_conv[0].transpose(1, 0, 2)

    def conv_sample(z):
        z_t = z.reshape(n_dec, n_new, d).transpose(1, 0, 2)
        return _conv_sample(st_t, z_t, w_dw).transpose(1, 0, 2).reshape(n_dec * n_new, d)

    def attn_sample(q, k, v, k_bf, v_bf):
        shp = (n_dec, n_new, d)
        o = _attn_sample(q.reshape(shp), k.reshape(shp), v.reshape(shp), cache_k, cache_v,
                         page_table, bias_past, bias_own)
        return o.reshape(n_dec * n_new, d)

    y_s, k_s, v_s, z_s = _trunk(
        x_sample.reshape(n_dec * n_new, d), mods_s, wts, mode="row", rows_per_seq=n_new,
        bm=n_dec * n_new, conv_fn=conv_sample, attn_fn=attn_sample)

    heads = (N_HEADS, HEAD_DIM)
    conv_p = z_p.reshape(n_seq, seq, d)[:, seq - (CONV_W - 1):][None]
    conv_s = jnp.concatenate([state_conv[0][:, n_new:], z_s.reshape(n_dec, n_new, d)], axis=1)[None]
    return (y_p.reshape(n_seq, seq, d), y_s.reshape(n_dec, n_new, d),
            k_p.reshape(n_seq, seq, *heads), v_p.reshape(n_seq, seq, *heads),
            k_s.reshape(n_dec, n_new, *heads), v_s.reshape(n_dec, n_new, *heads),
            conv_p, conv_s)
```

```python
import functools
import math

import jax
import jax.numpy as jnp
from jax import lax
from jax.experimental import pallas as pl
from jax.experimental.pallas import tpu as pltpu

F32 = jnp.float32
BF16 = jnp.bfloat16

D_MODEL = 2048
HEAD_DIM = 128
N_HEADS = D_MODEL // HEAD_DIM
CONV_W = 31
MOBA_BLOCK = 256
MOBA_TOPK = 3
REL_BUCKETS = 32
REL_MAX_DIST = 128
PAGE_SIZE = 128
NORM_EPS = 1e-6

SUBLANES = 8
CONV_HALO = 32
VMEM_LIMIT_BYTES = 56 * 1024 * 1024

_CONTRACT_LAST = (((1,), (1,)), ((), ()))


def _params(n_axes):
    return pltpu.CompilerParams(dimension_semantics=("arbitrary",) * n_axes,
                                vmem_limit_bytes=VMEM_LIMIT_BYTES)


def _sigmoid(x):
    return 1.0 / (1.0 + jnp.exp(-x))


def _head_rms(a, gain, scale):
    outs = []
    for c in range(a.shape[1] // HEAD_DIM):
        blk = a[:, c * HEAD_DIM:(c + 1) * HEAD_DIM]
        ms = jnp.mean(blk * blk, axis=-1, keepdims=True)
        y = blk * lax.rsqrt(ms + NORM_EPS) * gain
        if scale != 1.0:
            y = y * scale
        outs.append(y)
    return jnp.concatenate(outs, axis=1)


def _fused_mm_body(*refs, prologue, epilogue, n_w, n_out, head_scale, row_chunk):
    refs = list(refs)
    h_ref = refs.pop() if prologue is not None else None
    outs = refs[len(refs) - n_out:]
    it = iter(refs[:len(refs) - n_out])
    x_ref = next(it)
    if prologue == "rmsmod":
        g_ref, sh_ref, sc_ref = next(it), next(it), next(it)
    elif prologue == "lnsilu":
        cb_ref, lg_ref, lb_ref = next(it), next(it), next(it)
    w_refs = [next(it) for _ in range(n_w)]
    if epilogue == "bias":
        b_ref = next(it)
    elif epilogue == "resid":
        res_ref, gate_ref = next(it), next(it)
    elif epilogue in ("headnorm", "kv"):
        hg_ref = next(it)

    bm = x_ref.shape[0]

    if prologue is not None:
        @pl.when(pl.program_id(1) == 0)
        def _prologue():
            def vec(ref, r0):
                return ref[...] if ref.shape[0] == 1 else ref[pl.ds(r0, row_chunk), :]

            def chunk(c, carry):
                r0 = pl.multiple_of(c * row_chunk, row_chunk)
                x = x_ref[pl.ds(r0, row_chunk), :].astype(F32)
                if prologue == "rmsmod":
                    ms = jnp.mean(x * x, axis=-1, keepdims=True)
                    y = x * lax.rsqrt(ms + NORM_EPS) * g_ref[...]
                    h = y * (1.0 + vec(sc_ref, r0)) + vec(sh_ref, r0)
                else:
                    x = x + cb_ref[...]
                    mu = jnp.mean(x, axis=-1, keepdims=True)
                    xc = x - mu
                    y = xc * lax.rsqrt(jnp.mean(xc * xc, axis=-1, keepdims=True) + NORM_EPS)
                    y = y * lg_ref[...] + lb_ref[...]
                    h = y * _sigmoid(y)
                h_ref[pl.ds(r0, row_chunk), :] = h.astype(BF16)
                return carry

            lax.fori_loop(0, bm // row_chunk, chunk, 0)

        h = h_ref[...]
    else:
        h = x_ref[...].astype(BF16)

    accs = [jnp.dot(h, w[...].astype(BF16), preferred_element_type=F32) for w in w_refs]

    if epilogue == "bias":
        res = [accs[0] + b_ref[...]]
    elif epilogue == "glu":
        res = [accs[0] * _sigmoid(accs[1])]
    elif epilogue == "swiglu":
        res = [accs[0] * _sigmoid(accs[0]) * accs[1]]
    elif epilogue == "resid":
        res = [res_ref[...] + gate_ref[...] * accs[0]]
    elif epilogue == "headnorm":
        res = [_head_rms(accs[0], hg_ref[...], head_scale)]
    elif epilogue == "kv":
        k = _head_rms(accs[0], hg_ref[...], head_scale)
        res = [k, k, accs[1], accs[1]]
    else:
        res = [accs[0]]
    if len(res) == 1:
        res = res * n_out
    for o_ref, r in zip(outs, res):
        o_ref[...] = r.astype(o_ref.dtype)


class _Mod:
    def __init__(self, arr, chunk, mode, rows_per_seq):
        self.arr, self.chunk, self.mode, self.rows_per_seq = arr, chunk, mode, rows_per_seq

    def spec(self, bm, width, follow_n):
        per = D_MODEL // width
        base = self.chunk * per
        if self.mode == "seq":
            blocks_per_seq = self.rows_per_seq // bm
            return pl.BlockSpec(
                (None, 1, width),
                lambda i, j: (i // blocks_per_seq, 0, base + (j if follow_n else 0)))
        return pl.BlockSpec((bm, width), lambda i, j: (i, base + (j if follow_n else 0)))


def _fused_mm(x, ws, *, bm, bn, n_cols, out_dtypes, prologue=None, pro_args=(), epilogue=None,
              epi_args=(), head_scale=1.0, name):
    m, k = x.shape
    assert m % bm == 0 and n_cols % bn == 0
    grid = (m // bm, n_cols // bn)
    row_chunk = 128 if bm % 128 == 0 else bm

    in_specs = [pl.BlockSpec((bm, k), lambda i, j: (i, 0))]
    args = [x]
    full_vec = pl.BlockSpec((1, k), lambda i, j: (0, 0))
    if prologue == "rmsmod":
        g, sh, sc = pro_args
        in_specs += [full_vec, sh.spec(bm, k, False), sc.spec(bm, k, False)]
        args += [g.reshape(1, k), sh.arr, sc.arr]
    elif prologue == "lnsilu":
        in_specs += [full_vec] * 3
        args += [a.reshape(1, k) for a in pro_args]
    for w, col0 in ws:
        assert col0 % bn == 0
        off = col0 // bn
        in_specs.append(pl.BlockSpec((k, bn), lambda i, j, off=off: (0, off + j)))
        args.append(w)
    if epilogue == "bias":
        in_specs.append(pl.BlockSpec((1, bn), lambda i, j: (0, j)))
        args.append(epi_args[0].reshape(1, n_cols))
    elif epilogue == "resid":
        res, gate = epi_args
        in_specs += [pl.BlockSpec((bm, bn), lambda i, j: (i, j)), gate.spec(bm, bn, True)]
        args += [res, gate.arr]
    elif epilogue in ("headnorm", "kv"):
        in_specs.append(pl.BlockSpec((1, HEAD_DIM), lambda i, j: (0, 0)))
        args.append(epi_args[0].reshape(1, HEAD_DIM))

    out_shape = [jax.ShapeDtypeStruct((m, n_cols), dt) for dt in out_dtypes]
    out_specs = [pl.BlockSpec((bm, bn), lambda i, j: (i, j)) for _ in out_dtypes]
    scratch = [pltpu.VMEM((bm, k), BF16)] if prologue is not None else []
    body = functools.partial(_fused_mm_body, prologue=prologue, epilogue=epilogue, n_w=len(ws),
                             n_out=len(out_dtypes), head_scale=head_scale, row_chunk=row_chunk)
    out = pl.pallas_call(
        body, grid=grid, in_specs=in_specs, out_specs=out_specs, out_shape=out_shape,
        scratch_shapes=scratch, compiler_params=_params(2), name=name)(*args)
    return out[0] if len(out) == 1 else out


def _conv_prompt_body(z_ref, halo_ref, w_ref, o_ref, ext_ref, sh_ref, *, sub_rows):
    bt = z_ref.shape[0]
    tb = pl.program_id(1)

    @pl.when(tb == 0)
    def _():
        ext_ref[0:CONV_HALO, :] = jnp.zeros((CONV_HALO, ext_ref.shape[1]), F32)

    @pl.when(tb > 0)
    def _():
        ext_ref[0:CONV_HALO, :] = halo_ref[...]

    ext_ref[CONV_HALO:, :] = z_ref[...]
    n_sh = bt + CONV_HALO - SUBLANES
    for r in range(1, SUBLANES):
        sh_ref[r, 0:n_sh, :] = ext_ref[r:r + n_sh, :]
    first = CONV_HALO - (CONV_W - 1)
    for r0 in range(0, bt, sub_rows):
        acc = None
        for j in range(CONV_W):
            a, r = divmod(first + j, SUBLANES)
            lo = a * SUBLANES + r0
            rows = ext_ref[lo:lo + sub_rows, :] if r == 0 else sh_ref[r, lo:lo + sub_rows, :]
            term = w_ref[j:j + 1, :] * rows
            acc = term if acc is None else acc + term
        o_ref[r0:r0 + sub_rows, :] = acc


def _conv_prompt(z, w_dw, n_seq, seq):
    bt, bd = 512, 256
    t_blocks = seq // bt
    halo_per_seq = seq // CONV_HALO
    halo_per_blk = bt // CONV_HALO
    grid = (n_seq, t_blocks, D_MODEL // bd)
    return pl.pallas_call(
        functools.partial(_conv_prompt_body, sub_rows=32),
        grid=grid,
        in_specs=[
            pl.BlockSpec((bt, bd), lambda n, t, d: (n * t_blocks + t, d)),
            pl.BlockSpec((CONV_HALO, bd),
                         lambda n, t, d: (jnp.maximum(n * halo_per_seq + t * halo_per_blk - 1, 0), d)),
            pl.BlockSpec((CONV_W, bd), lambda n, t, d: (0, d)),
        ],
        out_specs=pl.BlockSpec((bt, bd), lambda n, t, d: (n * t_blocks + t, d)),
        out_shape=jax.ShapeDtypeStruct(z.shape, F32),
        scratch_shapes=[pltpu.VMEM((CONV_HALO + bt, bd), F32),
                        pltpu.VMEM((SUBLANES, CONV_HALO + bt, bd), F32)],
        compiler_params=_params(3), name="conv_prompt")(z, z, w_dw)


def _conv_sample_body(st_ref, z_ref, w_ref, o_ref):
    n_new = z_ref.shape[0]
    n_old = st_ref.shape[0]
    for t in range(n_new):
        acc = None
        for i in range(t, n_old):
            term = w_ref[i - t:i - t + 1, :] * st_ref[i]
            acc = term if acc is None else acc + term
        for s in range(t + 1):
            jw = n_old + s - t
            acc = acc + w_ref[jw:jw + 1, :] * z_ref[s]
        o_ref[t] = acc


def _conv_sample(st_t, z_t, w_dw):
    n_old, n_seq, _ = st_t.shape
    n_new = z_t.shape[0]
    bd = 512
    return pl.pallas_call(
        _conv_sample_body,
        grid=(D_MODEL // bd,),
        in_specs=[
            pl.BlockSpec((n_old, n_seq, bd), lambda d: (0, 0, d)),
            pl.BlockSpec((n_new, n_seq, bd), lambda d: (0, 0, d)),
            pl.BlockSpec((CONV_W, bd), lambda d: (0, d)),
        ],
        out_specs=pl.BlockSpec((n_new, n_seq, bd), lambda d: (0, 0, d)),
        out_shape=jax.ShapeDtypeStruct(z_t.shape, F32),
        compiler_params=_params(1), name="conv_sample")(st_t, z_t, w_dw)


def _rel_bucket(dist):
    n = jnp.maximum(dist, 0)
    max_exact = REL_BUCKETS // 2
    nf = jnp.maximum(n, 1).astype(F32)
    large = max_exact + (jnp.log(nf / max_exact) / math.log(REL_MAX_DIST / max_exact)
                         * (REL_BUCKETS - max_exact)).astype(jnp.int32)
    large = jnp.minimum(large, REL_BUCKETS - 1)
    return jnp.where(n < max_exact, n, large)


def _top_blocks_rows(gate, blk_ix, topk):
    cnt = jnp.zeros(gate.shape, jnp.int32)
    for b in range(gate.shape[0]):
        other = gate[b:b + 1, :]
        beats = (other > gate) | ((other == gate) & (b < blk_ix))
        cnt = cnt + beats.astype(jnp.int32)
    return cnt < topk


def _kmean_body(k_ref, o_ref):
    o_ref[...] = jnp.sum(k_ref[...], axis=0, keepdims=True) / MOBA_BLOCK


def _block_means(k, n_blocks):
    return pl.pallas_call(
        _kmean_body, grid=(n_blocks,),
        in_specs=[pl.BlockSpec((MOBA_BLOCK, D_MODEL), lambda b: (b, 0))],
        out_specs=pl.BlockSpec((None, 1, D_MODEL), lambda b: (b, 0, 0)),
        out_shape=jax.ShapeDtypeStruct((n_blocks, 1, D_MODEL), F32),
        compiler_params=_params(1), name="kmean_prompt")(k)


ATTN_HEADS_PER_STEP = 8


def _attn_prompt_body(q_ref, k_ref, vt_ref, km_ref, bias_ref, o_ref, sel_ref):
    blk = MOBA_BLOCK
    hd = HEAD_DIM
    n_blk = km_ref.shape[0]
    n_heads = q_ref.shape[1] // hd
    qb = pl.program_id(2)
    r_own = pl.multiple_of(qb * blk, blk)
    blk_ix = lax.broadcasted_iota(jnp.int32, (n_blk, n_heads * blk), 0)
    valid = blk_ix < qb

    heads = range(n_heads)
    cols = [slice(g * hd, (g + 1) * hd) for g in heads]
    q = [q_ref[:, cols[g]] for g in heads]
    qh = [q[g].astype(BF16) for g in heads]
    s_t = [lax.dot_general(k_ref[pl.ds(r_own, blk), cols[g]], qh[g], _CONTRACT_LAST,
                           preferred_element_type=F32) + bias_ref[g, 0] for g in heads]
    gate = [lax.dot_general(km_ref[:, cols[g]], q[g], _CONTRACT_LAST, precision=lax.Precision.HIGHEST,
                            preferred_element_type=F32) for g in heads]
    gate = jnp.where(valid, jnp.concatenate(gate, axis=1), -jnp.inf)
    sel_ref[...] = (valid & _top_blocks_rows(gate, blk_ix, MOBA_TOPK)).astype(F32)
    m = [jnp.max(s_t[g], axis=0, keepdims=True) for g in heads]
    p = [jnp.exp(s_t[g] - m[g]) for g in heads]
    l = [jnp.sum(p[g], axis=0, keepdims=True) for g in heads]
    acc = [jnp.dot(vt_ref[qb, cols[g], :], p[g].astype(BF16), preferred_element_type=F32)
           for g in heads]
    state = [(m[g], l[g], acc[g]) for g in heads]

    def body(b, state):
        r0 = pl.multiple_of(b * blk, blk)
        tile = jnp.minimum(qb - b, 2)
        s_t = [lax.dot_general(k_ref[pl.ds(r0, blk), cols[g]], qh[g], _CONTRACT_LAST,
                               preferred_element_type=F32) for g in heads]
        s_t = [jnp.where(sel_ref[pl.ds(b, 1), g * blk:(g + 1) * blk] > 0.0,
                         s_t[g] + bias_ref[g, tile], -jnp.inf) for g in heads]
        m_new = [jnp.maximum(state[g][0], jnp.max(s_t[g], axis=0, keepdims=True)) for g in heads]
        alpha = [jnp.exp(state[g][0] - m_new[g]) for g in heads]
        p = [jnp.exp(s_t[g] - m_new[g]) for g in heads]
        l = [alpha[g] * state[g][1] + jnp.sum(p[g], axis=0, keepdims=True) for g in heads]
        pv = [jnp.dot(vt_ref[b, cols[g], :], p[g].astype(BF16), preferred_element_type=F32)
              for g in heads]
        return [(m_new[g], l[g], alpha[g] * state[g][2] + pv[g]) for g in heads]

    state = lax.fori_loop(0, qb, body, state)
    for g in range(n_heads):
        m, l, acc = state[g]
        o_ref[:, g * hd:(g + 1) * hd] = (acc / l).T.astype(o_ref.dtype)


def _attn_prompt(q, k_bf, vt_bf, k_mean, bias_tiles, n_seq, seq):
    n_blk = seq // MOBA_BLOCK
    hps = ATTN_HEADS_PER_STEP
    w = hps * HEAD_DIM
    grid = (n_seq, N_HEADS // hps, n_blk)
    return pl.pallas_call(
        _attn_prompt_body, grid=grid,
        in_specs=[
            pl.BlockSpec((MOBA_BLOCK, w), lambda n, h, b: (n * n_blk + b, h)),
            pl.BlockSpec((seq, w), lambda n, h, b: (n, h)),
            pl.BlockSpec((None, n_blk, w, MOBA_BLOCK), lambda n, h, b: (n, 0, h, 0)),
            pl.BlockSpec((None, n_blk, w), lambda n, h, b: (n, 0, h)),
            pl.BlockSpec((hps, 3, MOBA_BLOCK, MOBA_BLOCK), lambda n, h, b: (h, 0, 0, 0)),
        ],
        out_specs=pl.BlockSpec((MOBA_BLOCK, w), lambda n, h, b: (n * n_blk + b, h)),
        out_shape=jax.ShapeDtypeStruct(q.shape, BF16),
        scratch_shapes=[pltpu.VMEM((n_blk, hps * MOBA_BLOCK), F32)],
        compiler_params=_params(3), name="attn_prompt")(q, k_bf, vt_bf, k_mean, bias_tiles)


def _attn_sample_body(pt_ref, q_ref, kn_ref, vn_ref, ck0_ref, ck1_ref, cv0_ref, cv1_ref, bp_ref, bo_ref,
                      o_ref, kblk_ref, m_ref, l_ref, acc_ref):
    del pt_ref
    rows = q_ref.shape[0]
    n_new = rows // N_HEADS
    n_full = kblk_ref.shape[0]
    b = pl.program_id(1)
    last = pl.num_programs(1) - 1

    qh = q_ref[...].astype(BF16)
    s, page_sum = [], []
    for i, ck_ref in enumerate((ck0_ref, ck1_ref)):
        kp = ck_ref[...]
        page_sum.append(jnp.sum(kp.reshape(PAGE_SIZE, N_HEADS, HEAD_DIM), axis=0))
        s.append(lax.dot_general(qh, kp.astype(BF16), _CONTRACT_LAST, preferred_element_type=F32)
                 + bp_ref[2 * b + i])
    m_blk = jnp.maximum(jnp.max(s[0], axis=-1, keepdims=True), jnp.max(s[1], axis=-1, keepdims=True))
    p0 = jnp.exp(s[0] - m_blk)
    p1 = jnp.exp(s[1] - m_blk)
    kblk_ref[b] = page_sum[0] + page_sum[1]
    m_ref[b] = m_blk
    l_ref[b] = jnp.sum(p0, axis=-1, keepdims=True) + jnp.sum(p1, axis=-1, keepdims=True)
    acc_ref[b] = (jnp.dot(p0.astype(BF16), cv0_ref[...].astype(BF16), preferred_element_type=F32)
                  + jnp.dot(p1.astype(BF16), cv1_ref[...].astype(BF16), preferred_element_type=F32))

    @pl.when(b == last)
    def _combine():
        q = q_ref[...]
        gates = []
        for c in range(n_full):
            k_mean = kblk_ref[c] / MOBA_BLOCK
            k_rows = jnp.concatenate([k_mean] * n_new, axis=0)
            gates.append(jnp.sum(q * k_rows, axis=-1, keepdims=True))
        sel = []
        for c in range(n_full):
            cnt = jnp.zeros((rows, 1), jnp.int32)
            for o in range(n_full):
                beats = gates[o] > gates[c]
                if o < c:
                    beats = beats | (gates[o] == gates[c])
                cnt = cnt + beats.astype(jnp.int32)
            sel.append(cnt < MOBA_TOPK)

        s_own = lax.dot_general(q, kn_ref[...], _CONTRACT_LAST, precision=lax.Precision.HIGHEST,
                                preferred_element_type=F32) + bo_ref[...]
        m = jnp.max(s_own, axis=-1, keepdims=True)
        for c in range(n_full):
            m = jnp.maximum(m, jnp.where(sel[c], m_ref[c], -jnp.inf))
        p_own = jnp.exp(s_own - m)
        l = jnp.sum(p_own, axis=-1, keepdims=True)
        acc = jnp.dot(p_own, vn_ref[...], precision=lax.Precision.HIGHEST,
                      preferred_element_type=F32)
        for c in range(n_full):
            w = jnp.where(sel[c], jnp.exp(m_ref[c] - m), 0.0)
            l = l + w * l_ref[c]
            acc = acc + w * acc_ref[c]
        o_ref[...] = acc / l


def _attn_sample(q, k_new, v_new, cache_k, cache_v, page_table, bias_past, bias_own):
    n_dec, rows, _ = q.shape
    n_pages = page_table.shape[1]
    n_full = n_pages * PAGE_SIZE // MOBA_BLOCK
    assert MOBA_BLOCK == 2 * PAGE_SIZE
    page_rows = PAGE_SIZE * N_HEADS
    seq_spec = pl.BlockSpec((None, rows, HEAD_DIM), lambda n, b, pt: (n, 0, 0))
    page0_spec = pl.BlockSpec((page_rows, HEAD_DIM), lambda n, b, pt: (pt[n, 2 * b], 0))
    page1_spec = pl.BlockSpec((page_rows, HEAD_DIM), lambda n, b, pt: (pt[n, 2 * b + 1], 0))
    grid_spec = pltpu.PrefetchScalarGridSpec(
        num_scalar_prefetch=1,
        grid=(n_dec, n_full),
        in_specs=[
            seq_spec, seq_spec, seq_spec, page0_spec, page1_spec, page0_spec, page1_spec,
            pl.BlockSpec((n_pages, rows, page_rows), lambda n, j, pt: (0, 0, 0)),
            pl.BlockSpec((rows, rows), lambda n, j, pt: (0, 0)),
        ],
        out_specs=seq_spec,
        scratch_shapes=[
            pltpu.VMEM((n_full, N_HEADS, HEAD_DIM), F32),
            pltpu.VMEM((n_full, rows, 1), F32),
            pltpu.VMEM((n_full, rows, 1), F32),
            pltpu.VMEM((n_full, rows, HEAD_DIM), F32),
        ])
    ck = cache_k.reshape(-1, HEAD_DIM)
    cv = cache_v.reshape(-1, HEAD_DIM)
    return pl.pallas_call(
        _attn_sample_body, grid_spec=grid_spec,
        out_shape=jax.ShapeDtypeStruct(q.shape, F32),
        compiler_params=_params(2), name="attn_sample")(
            page_table, q, k_new, v_new, ck, ck, cv, cv, bias_past, bias_own)


def _bias_of_dist(rel_bias, dist):
    onehot = _rel_bucket(dist)[..., None] == jnp.arange(REL_BUCKETS)
    return jnp.sum(jnp.where(onehot[..., None], rel_bias, 0.0), axis=-2)


def _toeplitz(w, n):
    h = w.shape[0]
    wp = jnp.concatenate([w, jnp.zeros((h, 1), w.dtype)], axis=1)
    x = jnp.tile(wp, (1, n))[:, :n * (2 * n - 1)].reshape(h, n, 2 * n - 1)
    return x[:, :, n - 1:]


def _prompt_bias_tiles(rel_bias):
    assert REL_MAX_DIST <= MOBA_BLOCK
    n = MOBA_BLOCK
    tiles = []
    for delta in range(3):
        dist = delta * n - (n - 1) + jnp.arange(2 * n - 1)
        w = jnp.where((dist >= 0)[:, None], _bias_of_dist(rel_bias, dist), -jnp.inf)
        tiles.append(_toeplitz(w.T, n))
    return jnp.stack(tiles, axis=1)


def _sample_bias_tables(rel_bias, past, n_new):
    rows = n_new * N_HEADS
    n_pages = past // PAGE_SIZE
    same_head = jnp.eye(N_HEADS, dtype=bool)
    qpos = past + jnp.arange(n_new)
    dist = qpos[:, None] - jnp.arange(past)[None, :]
    bp = _bias_of_dist(rel_bias, dist)
    bp = bp.reshape(n_new, n_pages, PAGE_SIZE, N_HEADS).transpose(1, 0, 3, 2)
    bp = jnp.where(same_head[None, None, :, None, :], bp[..., None], -jnp.inf)
    bp = bp.reshape(n_pages, rows, PAGE_SIZE * N_HEADS)
    d_own = jnp.arange(n_new)[:, None] - jnp.arange(n_new)[None, :]
    bo = jnp.where((d_own >= 0)[..., None], _bias_of_dist(rel_bias, d_own), -jnp.inf)
    bo = bo.transpose(0, 2, 1)
    bo = jnp.where(same_head[None, :, None, :], bo[..., None], -jnp.inf)
    return bp, bo.reshape(rows, rows)


def _trunk(x, mods, wts, *, mode, rows_per_seq, bm, conv_fn, attn_fn):
    mod0, mod1, modkv = mods
    M = lambda arr, c: _Mod(arr, c, mode, rows_per_seq)
    d_ff = wts["ffn_gate"][0].shape[1]
    bn = 512

    def ffn(xin, layer, mod):
        gu = _fused_mm(xin, [(wts["ffn_gate"][layer], 0), (wts["ffn_up"][layer], 0)], bm=bm, bn=bn,
                       n_cols=d_ff, out_dtypes=[BF16], prologue="rmsmod",
                       pro_args=(wts["norm_ffn_g"][layer], M(mod, 3), M(mod, 4)),
                       epilogue="swiglu", name=f"ffn_up{layer}_{mode}")
        return _fused_mm(gu, [(wts["ffn_down"][layer], 0)], bm=min(bm, 512), bn=bn, n_cols=D_MODEL,
                         out_dtypes=[F32], epilogue="resid", epi_args=(xin, M(mod, 5)),
                         name=f"ffn_down{layer}_{mode}")

    z = _fused_mm(x, [(wts["conv_in"], 0), (wts["conv_in"], D_MODEL)], bm=bm, bn=bn, n_cols=D_MODEL,
                  out_dtypes=[F32], prologue="rmsmod",
                  pro_args=(wts["norm_mix_g"][0], M(mod0, 0), M(mod0, 1)), epilogue="glu",
                  name=f"conv_in_{mode}")
    yc = conv_fn(z)
    x = _fused_mm(yc, [(wts["conv_out"], 0)], bm=bm, bn=bn, n_cols=D_MODEL, out_dtypes=[F32],
                  prologue="lnsilu", pro_args=(wts["conv_b_dw"], wts["conv_ln_g"], wts["conv_ln_b"]),
                  epilogue="resid", epi_args=(x, M(mod0, 2)), name=f"conv_out_{mode}")
    x = ffn(x, 0, mod0)

    k, k_bf, v, v_bf = _fused_mm(
        x, [(wts["kv_k"], 0), (wts["kv_v"], 0)], bm=bm, bn=bn, n_cols=D_MODEL,
        out_dtypes=[F32, BF16, F32, BF16], prologue="rmsmod",
        pro_args=(wts["norm_kv_g"], M(modkv, 0), M(modkv, 1)), epilogue="kv",
        epi_args=(wts["kv_g_k"],), name=f"kv_{mode}")

    q = _fused_mm(x, [(wts["attn_q"], 0)], bm=bm, bn=bn, n_cols=D_MODEL, out_dtypes=[F32],
                  prologue="rmsmod", pro_args=(wts["norm_mix_g"][1], M(mod1, 0), M(mod1, 1)),
                  epilogue="headnorm", epi_args=(wts["attn_g_q"],), head_scale=HEAD_DIM ** -0.5,
                  name=f"q_{mode}")
    o = attn_fn(q, k, v, k_bf, v_bf)
    x = _fused_mm(o, [(wts["attn_o"], 0)], bm=bm, bn=bn, n_cols=D_MODEL, out_dtypes=[F32],
                  epilogue="resid", epi_args=(x, M(mod1, 2)), name=f"attn_o_{mode}")
    x = ffn(x, 1, mod1)
    return x, k, v, z


def kernel(x_prompt, x_sample, c_prompt, c_sample, cache_k, cache_v, state_conv, page_table, ada_w, ada_b, norm_mix_g, norm_ffn_g, conv_w_in, conv_w_dw, conv_b_dw, conv_ln_g, conv_ln_b, conv_w_out, kv_ada_w, kv_ada_b, norm_kv_g, kv_w_k, kv_w_v, kv_g_k, attn_w_q, attn_g_q, attn_w_o, rel_bias, ffn_w_gate, ffn_w_up, ffn_w_down):
    n_seq, seq, d = x_prompt.shape
    n_dec, n_new, _ = x_sample.shape
    assert d == D_MODEL and ada_w.shape[0] == 2 and state_conv.shape[0] == 1
    past = page_table.shape[1] * PAGE_SIZE
    assert past % MOBA_BLOCK == 0 and seq % MOBA_BLOCK == 0

    n_c = n_seq + n_dec
    c_rows = -(-n_c // 16) * 16
    c_all = jnp.concatenate([c_prompt, c_sample, jnp.zeros((c_rows - n_c, d), F32)], axis=0)

    def ada(w, b, name):
        return _fused_mm(c_all, [(w, 0)], bm=c_rows, bn=512, n_cols=w.shape[1], out_dtypes=[F32],
                         epilogue="bias", epi_args=(b,), name=name)

    mods = [ada(ada_w[0], ada_b[0], "ada0"), ada(ada_w[1], ada_b[1], "ada1"),
            ada(kv_ada_w, kv_ada_b, "ada_kv")]
    mods_p = [m[:n_seq].reshape(n_seq, 1, -1) for m in mods]
    mods_s = [jnp.repeat(m[n_seq:n_c], n_new, axis=0) for m in mods]

    wts = {
        "norm_mix_g": norm_mix_g, "norm_ffn_g": norm_ffn_g,
        "conv_in": conv_w_in[0].astype(BF16), "conv_out": conv_w_out[0].astype(BF16),
        "conv_b_dw": conv_b_dw[0], "conv_ln_g": conv_ln_g[0], "conv_ln_b": conv_ln_b[0],
        "norm_kv_g": norm_kv_g, "kv_k": kv_w_k.astype(BF16), "kv_v": kv_w_v.astype(BF16),
        "kv_g_k": kv_g_k, "attn_q": attn_w_q[0].astype(BF16), "attn_g_q": attn_g_q[0],
        "attn_o": attn_w_o[0].astype(BF16),
        "ffn_gate": ffn_w_gate.astype(BF16), "ffn_up": ffn_w_up.astype(BF16),
        "ffn_down": ffn_w_down.astype(BF16),
    }
    w_dw = conv_w_dw[0]

    bias_tiles = _prompt_bias_tiles(rel_bias)

    def attn_prompt(q, k, v, k_bf, v_bf):
        n_blk = seq // MOBA_BLOCK
        k_mean = _block_means(k, n_seq * n_blk).reshape(n_seq, n_blk, D_MODEL)
        vt_bf = v_bf.reshape(n_seq, n_blk, MOBA_BLOCK, D_MODEL).transpose(0, 1, 3, 2)
        return _attn_prompt(q, k_bf, vt_bf, k_mean, bias_tiles, n_seq, seq)

    y_p, k_p, v_p, z_p = _trunk(
        x_prompt.reshape(n_seq * seq, d), mods_p, wts, mode="seq", rows_per_seq=seq, bm=1024,
        conv_fn=lambda z: _conv_prompt(z, w_dw, n_seq, seq), attn_fn=attn_prompt)

    bias_past, bias_own = _sample_bias_tables(rel_bias, past, n_new)
    st_t = state_conv[0].transpose(1, 0, 2)

    def conv_sample(z):
        z_t = z.reshape(n_dec, n_new, d).transpose(1, 0, 2)
        return _conv_sample(st_t, z_t, w_dw).transpose(1, 0, 2).reshape(n_dec * n_new, d)

    def attn_sample(q, k, v, k_bf, v_bf):
        shp = (n_dec, n_new * N_HEADS, HEAD_DIM)
        o = _attn_sample(q.reshape(shp), k.reshape(shp), v.reshape(shp), cache_k, cache_v,
                         page_table, bias_past, bias_own)
        return o.reshape(n_dec * n_new, d)

    y_s, k_s, v_s, z_s = _trunk(
        x_sample.reshape(n_dec * n_new, d), mods_s, wts, mode="row", rows_per_seq=n_new,
        bm=n_dec * n_new, conv_fn=conv_sample, attn_fn=attn_sample)

    heads = (N_HEADS, HEAD_DIM)
    conv_p = z_p.reshape(n_seq, seq, d)[:, seq - (CONV_W - 1):][None]
    conv_s = jnp.concatenate([state_conv[0][:, n_new:], z_s.reshape(n_dec, n_new, d)], axis=1)[None]
    return (y_p.reshape(n_seq, seq, d), y_s.reshape(n_dec, n_new, d),
            k_p.reshape(n_seq, seq, *heads), v_p.reshape(n_seq, seq, *heads),
            k_s.reshape(n_dec, n_new, *heads), v_s.reshape(n_dec, n_new, *heads),
            conv_p, conv_s)
```

```python
import functools
import math

import jax
import jax.numpy as jnp
from jax import lax
from jax.experimental import pallas as pl
from jax.experimental.pallas import tpu as pltpu

F32 = jnp.float32
BF16 = jnp.bfloat16

D_MODEL = 2048
HEAD_DIM = 128
N_HEADS = D_MODEL // HEAD_DIM
CONV_W = 31
MOBA_BLOCK = 256
MOBA_TOPK = 3
REL_BUCKETS = 32
REL_MAX_DIST = 128
PAGE_SIZE = 128
NORM_EPS = 1e-6

SUBLANES = 8
CONV_HALO = 32
VMEM_LIMIT_BYTES = 56 * 1024 * 1024

_CONTRACT_LAST = (((1,), (1,)), ((), ()))


def _params(n_axes):
    return pltpu.CompilerParams(dimension_semantics=("arbitrary",) * n_axes,
                                vmem_limit_bytes=VMEM_LIMIT_BYTES)


def _sigmoid(x):
    return 1.0 / (1.0 + jnp.exp(-x))


def _head_rms(a, gain, scale):
    outs = []
    for c in range(a.shape[1] // HEAD_DIM):
        blk = a[:, c * HEAD_DIM:(c + 1) * HEAD_DIM]
        ms = jnp.mean(blk * blk, axis=-1, keepdims=True)
        y = blk * lax.rsqrt(ms + NORM_EPS) * gain
        if scale != 1.0:
            y = y * scale
        outs.append(y)
    return jnp.concatenate(outs, axis=1)


def _fused_mm_body(*refs, prologue, epilogue, n_w, n_out, head_scale, row_chunk):
    refs = list(refs)
    h_ref = refs.pop() if prologue is not None else None
    outs = refs[len(refs) - n_out:]
    it = iter(refs[:len(refs) - n_out])
    x_ref = next(it)
    if prologue == "rmsmod":
        g_ref, sh_ref, sc_ref = next(it), next(it), next(it)
    elif prologue == "lnsilu":
        cb_ref, lg_ref, lb_ref = next(it), next(it), next(it)
    w_refs = [next(it) for _ in range(n_w)]
    if epilogue == "bias":
        b_ref = next(it)
    elif epilogue == "resid":
        res_ref, gate_ref = next(it), next(it)
    elif epilogue in ("headnorm", "kv"):
        hg_ref = next(it)

    bm = x_ref.shape[0]

    if prologue is not None:
        @pl.when(pl.program_id(1) == 0)
        def _prologue():
            def vec(ref, r0):
                return ref[...] if ref.shape[0] == 1 else ref[pl.ds(r0, row_chunk), :]

            def chunk(c, carry):
                r0 = pl.multiple_of(c * row_chunk, row_chunk)
                x = x_ref[pl.ds(r0, row_chunk), :].astype(F32)
                if prologue == "rmsmod":
                    ms = jnp.mean(x * x, axis=-1, keepdims=True)
                    y = x * lax.rsqrt(ms + NORM_EPS) * g_ref[...]
                    h = y * (1.0 + vec(sc_ref, r0)) + vec(sh_ref, r0)
                else:
                    x = x + cb_ref[...]
                    mu = jnp.mean(x, axis=-1, keepdims=True)
                    xc = x - mu
                    y = xc * lax.rsqrt(jnp.mean(xc * xc, axis=-1, keepdims=True) + NORM_EPS)
                    y = y * lg_ref[...] + lb_ref[...]
                    h = y * _sigmoid(y)
                h_ref[pl.ds(r0, row_chunk), :] = h.astype(BF16)
                return carry

            lax.fori_loop(0, bm // row_chunk, chunk, 0)

        h = h_ref[...]
    else:
        h = x_ref[...].astype(BF16)

    accs = [jnp.dot(h, w[...].astype(BF16), preferred_element_type=F32) for w in w_refs]

    if epilogue == "bias":
        res = [accs[0] + b_ref[...]]
    elif epilogue == "glu":
        res = [accs[0] * _sigmoid(accs[1])]
    elif epilogue == "swiglu":
        res = [accs[0] * _sigmoid(accs[0]) * accs[1]]
    elif epilogue == "resid":
        res = [res_ref[...] + gate_ref[...] * accs[0]]
    elif epilogue == "headnorm":
        res = [_head_rms(accs[0], hg_ref[...], head_scale)]
    elif epilogue == "kv":
        k = _head_rms(accs[0], hg_ref[...], head_scale)
        res = [k, k, accs[1], accs[1]]
    else:
        res = [accs[0]]
    if len(res) == 1:
        res = res * n_out
    for o_ref, r in zip(outs, res):
        o_ref[...] = r.astype(o_ref.dtype)


class _Mod:
    def __init__(self, arr, chunk, mode, rows_per_seq):
        self.arr, self.chunk, self.mode, self.rows_per_seq = arr, chunk, mode, rows_per_seq

    def spec(self, bm, width, follow_n):
        per = D_MODEL // width
        base = self.chunk * per
        if self.mode == "seq":
            blocks_per_seq = self.rows_per_seq // bm
            return pl.BlockSpec(
                (None, 1, width),
                lambda i, j: (i // blocks_per_seq, 0, base + (j if follow_n else 0)))
        return pl.BlockSpec((bm, width), lambda i, j: (i, base + (j if follow_n else 0)))


def _fused_mm(x, ws, *, bm, bn, n_cols, out_dtypes, prologue=None, pro_args=(), epilogue=None,
              epi_args=(), head_scale=1.0, name):
    m, k = x.shape
    assert m % bm == 0 and n_cols % bn == 0
    grid = (m // bm, n_cols // bn)
    row_chunk = 128 if bm % 128 == 0 else bm

    in_specs = [pl.BlockSpec((bm, k), lambda i, j: (i, 0))]
    args = [x]
    full_vec = pl.BlockSpec((1, k), lambda i, j: (0, 0))
    if prologue == "rmsmod":
        g, sh, sc = pro_args
        in_specs += [full_vec, sh.spec(bm, k, False), sc.spec(bm, k, False)]
        args += [g.reshape(1, k), sh.arr, sc.arr]
    elif prologue == "lnsilu":
        in_specs += [full_vec] * 3
        args += [a.reshape(1, k) for a in pro_args]
    for w, col0 in ws:
        assert col0 % bn == 0
        off = col0 // bn
        in_specs.append(pl.BlockSpec((k, bn), lambda i, j, off=off: (0, off + j)))
        args.append(w)
    if epilogue == "bias":
        in_specs.append(pl.BlockSpec((1, bn), lambda i, j: (0, j)))
        args.append(epi_args[0].reshape(1, n_cols))
    elif epilogue == "resid":
        res, gate = epi_args
        in_specs += [pl.BlockSpec((bm, bn), lambda i, j: (i, j)), gate.spec(bm, bn, True)]
        args += [res, gate.arr]
    elif epilogue in ("headnorm", "kv"):
        in_specs.append(pl.BlockSpec((1, HEAD_DIM), lambda i, j: (0, 0)))
        args.append(epi_args[0].reshape(1, HEAD_DIM))

    out_shape = [jax.ShapeDtypeStruct((m, n_cols), dt) for dt in out_dtypes]
    out_specs = [pl.BlockSpec((bm, bn), lambda i, j: (i, j)) for _ in out_dtypes]
    scratch = [pltpu.VMEM((bm, k), BF16)] if prologue is not None else []
    body = functools.partial(_fused_mm_body, prologue=prologue, epilogue=epilogue, n_w=len(ws),
                             n_out=len(out_dtypes), head_scale=head_scale, row_chunk=row_chunk)
    out = pl.pallas_call(
        body, grid=grid, in_specs=in_specs, out_specs=out_specs, out_shape=out_shape,
        scratch_shapes=scratch, compiler_params=_params(2), name=name)(*args)
    return out[0] if len(out) == 1 else out


def _conv_prompt_body(z_ref, halo_ref, w_ref, o_ref, ext_ref, sh_ref, *, sub_rows):
    bt = z_ref.shape[0]
    tb = pl.program_id(1)

    @pl.when(tb == 0)
    def _():
        ext_ref[0:CONV_HALO, :] = jnp.zeros((CONV_HALO, ext_ref.shape[1]), F32)

    @pl.when(tb > 0)
    def _():
        ext_ref[0:CONV_HALO, :] = halo_ref[...]

    ext_ref[CONV_HALO:, :] = z_ref[...]
    n_sh = bt + CONV_HALO - SUBLANES
    for r in range(1, SUBLANES):
        sh_ref[r, 0:n_sh, :] = ext_ref[r:r + n_sh, :]
    first = CONV_HALO - (CONV_W - 1)
    for r0 in range(0, bt, sub_rows):
        acc = None
        for j in range(CONV_W):
            a, r = divmod(first + j, SUBLANES)
            lo = a * SUBLANES + r0
            rows = ext_ref[lo:lo + sub_rows, :] if r == 0 else sh_ref[r, lo:lo + sub_rows, :]
            term = w_ref[j:j + 1, :] * rows
            acc = term if acc is None else acc + term
        o_ref[r0:r0 + sub_rows, :] = acc


def _conv_prompt(z, w_dw, n_seq, seq):
    bt, bd = 512, 256
    t_blocks = seq // bt
    halo_per_seq = seq // CONV_HALO
    halo_per_blk = bt // CONV_HALO
    grid = (n_seq, t_blocks, D_MODEL // bd)
    return pl.pallas_call(
        functools.partial(_conv_prompt_body, sub_rows=32),
        grid=grid,
        in_specs=[
            pl.BlockSpec((bt, bd), lambda n, t, d: (n * t_blocks + t, d)),
            pl.BlockSpec((CONV_HALO, bd),
                         lambda n, t, d: (jnp.maximum(n * halo_per_seq + t * halo_per_blk - 1, 0), d)),
            pl.BlockSpec((CONV_W, bd), lambda n, t, d: (0, d)),
        ],
        out_specs=pl.BlockSpec((bt, bd), lambda n, t, d: (n * t_blocks + t, d)),
        out_shape=jax.ShapeDtypeStruct(z.shape, F32),
        scratch_shapes=[pltpu.VMEM((CONV_HALO + bt, bd), F32),
                        pltpu.VMEM((SUBLANES, CONV_HALO + bt, bd), F32)],
        compiler_params=_params(3), name="conv_prompt")(z, z, w_dw)


def _conv_sample_body(st_ref, z_ref, wt_ref, wz_ref, y_ref, ns_ref):
    st = st_ref[...]
    z = z_ref[...]
    n_old, n_new = st.shape[1], z.shape[1]
    for t in range(n_new):
        y_ref[t] = jnp.sum(st * wt_ref[t][None], axis=1) + jnp.sum(z * wz_ref[t][None], axis=1)
    ns_ref[:, 0:n_old - n_new, :] = st[:, n_new:, :]
    ns_ref[:, n_old - n_new:, :] = z


def _conv_sample(st, z, w_dw):
    n_seq, n_old, _ = st.shape
    n_new = z.shape[1]
    bd = 256
    wt = jnp.stack([jnp.pad(w_dw[:n_old - t], ((t, 0), (0, 0))) for t in range(n_new)])
    wz = jnp.stack([jnp.pad(w_dw[n_old - t:], ((0, n_new - 1 - t), (0, 0))) for t in range(n_new)])
    blk = lambda rows: pl.BlockSpec((n_seq, rows, bd), lambda d: (0, 0, d))
    taps = lambda rows: pl.BlockSpec((n_new, rows, bd), lambda d: (0, 0, d))
    return pl.pallas_call(
        _conv_sample_body,
        grid=(D_MODEL // bd,),
        in_specs=[blk(n_old), blk(n_new), taps(n_old), taps(n_new)],
        out_specs=[pl.BlockSpec((n_new, n_seq, bd), lambda d: (0, 0, d)), blk(n_old)],
        out_shape=[jax.ShapeDtypeStruct((n_new, n_seq, D_MODEL), F32),
                   jax.ShapeDtypeStruct(st.shape, F32)],
        compiler_params=_params(1), name="conv_sample")(st, z, wt, wz)


def _rel_bucket(dist):
    n = jnp.maximum(dist, 0)
    max_exact = REL_BUCKETS // 2
    nf = jnp.maximum(n, 1).astype(F32)
    large = max_exact + (jnp.log(nf / max_exact) / math.log(REL_MAX_DIST / max_exact)
                         * (REL_BUCKETS - max_exact)).astype(jnp.int32)
    large = jnp.minimum(large, REL_BUCKETS - 1)
    return jnp.where(n < max_exact, n, large)


def _top_blocks_rows(gate, blk_ix, topk):
    cnt = jnp.zeros(gate.shape, jnp.int32)
    for b in range(gate.shape[0]):
        other = gate[b:b + 1, :]
        beats = (other > gate) | ((other == gate) & (b < blk_ix))
        cnt = cnt + beats.astype(jnp.int32)
    return cnt < topk


def _kmean_body(k_ref, o_ref):
    o_ref[...] = jnp.sum(k_ref[...], axis=0, keepdims=True) / MOBA_BLOCK


def _block_means(k, n_blocks):
    return pl.pallas_call(
        _kmean_body, grid=(n_blocks,),
        in_specs=[pl.BlockSpec((MOBA_BLOCK, D_MODEL), lambda b: (b, 0))],
        out_specs=pl.BlockSpec((None, 1, D_MODEL), lambda b: (b, 0, 0)),
        out_shape=jax.ShapeDtypeStruct((n_blocks, 1, D_MODEL), F32),
        compiler_params=_params(1), name="kmean_prompt")(k)


ATTN_HEADS_PER_STEP = 8


def _attn_prompt_body(q_ref, k_ref, vt_ref, km_ref, bias_ref, o_ref, sel_ref):
    blk = MOBA_BLOCK
    hd = HEAD_DIM
    n_blk = km_ref.shape[0]
    n_heads = q_ref.shape[1] // hd
    qb = pl.program_id(2)
    r_own = pl.multiple_of(qb * blk, blk)
    blk_ix = lax.broadcasted_iota(jnp.int32, (n_blk, n_heads * blk), 0)
    valid = blk_ix < qb

    heads = range(n_heads)
    cols = [slice(g * hd, (g + 1) * hd) for g in heads]
    q = [q_ref[:, cols[g]] for g in heads]
    qh = [q[g].astype(BF16) for g in heads]
    s_t = [lax.dot_general(k_ref[pl.ds(r_own, blk), cols[g]], qh[g], _CONTRACT_LAST,
                           preferred_element_type=F32) + bias_ref[g, 0] for g in heads]
    gate = [lax.dot_general(km_ref[:, cols[g]], q[g], _CONTRACT_LAST, precision=lax.Precision.HIGHEST,
                            preferred_element_type=F32) for g in heads]
    gate = jnp.where(valid, jnp.concatenate(gate, axis=1), -jnp.inf)
    sel_ref[...] = (valid & _top_blocks_rows(gate, blk_ix, MOBA_TOPK)).astype(F32)
    m = [jnp.max(s_t[g], axis=0, keepdims=True) for g in heads]
    p = [jnp.exp(s_t[g] - m[g]) for g in heads]
    l = [jnp.sum(p[g], axis=0, keepdims=True) for g in heads]
    acc = [jnp.dot(vt_ref[qb, cols[g], :], p[g].astype(BF16), preferred_element_type=F32)
           for g in heads]
    state = [(m[g], l[g], acc[g]) for g in heads]

    def body(b, state):
        r0 = pl.multiple_of(b * blk, blk)
        tile = jnp.minimum(qb - b, 2)
        s_t = [lax.dot_general(k_ref[pl.ds(r0, blk), cols[g]], qh[g], _CONTRACT_LAST,
                               preferred_element_type=F32) for g in heads]
        s_t = [jnp.where(sel_ref[pl.ds(b, 1), g * blk:(g + 1) * blk] > 0.0,
                         s_t[g] + bias_ref[g, tile], -jnp.inf) for g in heads]
        m_new = [jnp.maximum(state[g][0], jnp.max(s_t[g], axis=0, keepdims=True)) for g in heads]
        alpha = [jnp.exp(state[g][0] - m_new[g]) for g in heads]
        p = [jnp.exp(s_t[g] - m_new[g]) for g in heads]
        l = [alpha[g] * state[g][1] + jnp.sum(p[g], axis=0, keepdims=True) for g in heads]
        pv = [jnp.dot(vt_ref[b, cols[g], :], p[g].astype(BF16), preferred_element_type=F32)
              for g in heads]
        return [(m_new[g], l[g], alpha[g] * state[g][2] + pv[g]) for g in heads]

    state = lax.fori_loop(0, qb, body, state)
    for g in range(n_heads):
        m, l, acc = state[g]
        o_ref[:, g * hd:(g + 1) * hd] = (acc / l).T.astype(o_ref.dtype)


def _attn_prompt(q, k_bf, vt_bf, k_mean, bias_tiles, n_seq, seq):
    n_blk = seq // MOBA_BLOCK
    hps = ATTN_HEADS_PER_STEP
    w = hps * HEAD_DIM
    grid = (n_seq, N_HEADS // hps, n_blk)
    return pl.pallas_call(
        _attn_prompt_body, grid=grid,
        in_specs=[
            pl.BlockSpec((MOBA_BLOCK, w), lambda n, h, b: (n * n_blk + b, h)),
            pl.BlockSpec((seq, w), lambda n, h, b: (n, h)),
            pl.BlockSpec((None, n_blk, w, MOBA_BLOCK), lambda n, h, b: (n, 0, h, 0)),
            pl.BlockSpec((None, n_blk, w), lambda n, h, b: (n, 0, h)),
            pl.BlockSpec((hps, 3, MOBA_BLOCK, MOBA_BLOCK), lambda n, h, b: (h, 0, 0, 0)),
        ],
        out_specs=pl.BlockSpec((MOBA_BLOCK, w), lambda n, h, b: (n * n_blk + b, h)),
        out_shape=jax.ShapeDtypeStruct(q.shape, BF16),
        scratch_shapes=[pltpu.VMEM((n_blk, hps * MOBA_BLOCK), F32)],
        compiler_params=_params(3), name="attn_prompt")(q, k_bf, vt_bf, k_mean, bias_tiles)


SAMPLE_BLOCKS_PER_STEP = 2
PAGES_PER_BLOCK = MOBA_BLOCK // PAGE_SIZE


def _attn_sample_body(pt_ref, q_ref, kn_ref, vn_ref, *refs):
    del pt_ref
    n_pg = SAMPLE_BLOCKS_PER_STEP * PAGES_PER_BLOCK
    ck_refs, cv_refs = refs[:n_pg], refs[n_pg:2 * n_pg]
    bp_ref, bo_ref, o_ref, kblk_ref, m_ref, l_ref, acc_ref = refs[2 * n_pg:]
    rows = q_ref.shape[0]
    n_new = rows // N_HEADS
    n_full = kblk_ref.shape[0]
    step = pl.program_id(1)
    last = pl.num_programs(1) - 1

    qh = q_ref[...].astype(BF16)
    pages = range(n_pg)
    kp = [ck_refs[i][...] for i in pages]
    s = [lax.dot_general(qh, kp[i].astype(BF16), _CONTRACT_LAST, preferred_element_type=F32)
         + bp_ref[n_pg * step + i] for i in pages]
    page_sum = [jnp.sum(kp[i].reshape(PAGE_SIZE, N_HEADS, HEAD_DIM), axis=0) for i in pages]
    page_max = [jnp.max(s[i], axis=-1, keepdims=True) for i in pages]
    m_blk = [functools.reduce(jnp.maximum, page_max[c * PAGES_PER_BLOCK:(c + 1) * PAGES_PER_BLOCK])
             for c in range(SAMPLE_BLOCKS_PER_STEP)]
    p = [jnp.exp(s[i] - m_blk[i // PAGES_PER_BLOCK]) for i in pages]
    p_sum = [jnp.sum(p[i], axis=-1, keepdims=True) for i in pages]
    pv = [jnp.dot(p[i].astype(BF16), cv_refs[i][...].astype(BF16), preferred_element_type=F32)
          for i in pages]
    for c in range(SAMPLE_BLOCKS_PER_STEP):
        b = SAMPLE_BLOCKS_PER_STEP * step + c
        of_blk = slice(c * PAGES_PER_BLOCK, (c + 1) * PAGES_PER_BLOCK)
        kblk_ref[b] = functools.reduce(jnp.add, page_sum[of_blk])
        m_ref[b] = m_blk[c]
        l_ref[b] = functools.reduce(jnp.add, p_sum[of_blk])
        acc_ref[b] = functools.reduce(jnp.add, pv[of_blk])

    @pl.when(step == last)
    def _combine():
        q = q_ref[...]
        gates = []
        for c in range(n_full):
            k_mean = kblk_ref[c] / MOBA_BLOCK
            k_rows = jnp.concatenate([k_mean] * n_new, axis=0)
            gates.append(jnp.sum(q * k_rows, axis=-1, keepdims=True))
        sel = []
        for c in range(n_full):
            cnt = jnp.zeros((rows, 1), jnp.int32)
            for o in range(n_full):
                beats = gates[o] > gates[c]
                if o < c:
                    beats = beats | (gates[o] == gates[c])
                cnt = cnt + beats.astype(jnp.int32)
            sel.append(cnt < MOBA_TOPK)

        s_own = lax.dot_general(q, kn_ref[...], _CONTRACT_LAST, precision=lax.Precision.HIGHEST,
                                preferred_element_type=F32) + bo_ref[...]
        m = jnp.max(s_own, axis=-1, keepdims=True)
        for c in range(n_full):
            m = jnp.maximum(m, jnp.where(sel[c], m_ref[c], -jnp.inf))
        p_own = jnp.exp(s_own - m)
        l = jnp.sum(p_own, axis=-1, keepdims=True)
        acc = jnp.dot(p_own, vn_ref[...], precision=lax.Precision.HIGHEST,
                      preferred_element_type=F32)
        for c in range(n_full):
            w = jnp.where(sel[c], jnp.exp(m_ref[c] - m), 0.0)
            l = l + w * l_ref[c]
            acc = acc + w * acc_ref[c]
        o_ref[...] = acc / l


def _attn_sample(q, k_new, v_new, cache_k, cache_v, page_table, bias_past, bias_own):
    n_dec, rows, _ = q.shape
    n_pages = page_table.shape[1]
    n_full = n_pages * PAGE_SIZE // MOBA_BLOCK
    n_pg = SAMPLE_BLOCKS_PER_STEP * PAGES_PER_BLOCK
    assert n_pages % n_pg == 0
    page_rows = PAGE_SIZE * N_HEADS
    seq_spec = pl.BlockSpec((None, rows, HEAD_DIM), lambda n, s, pt: (n, 0, 0))
    page_specs = [pl.BlockSpec((page_rows, HEAD_DIM), lambda n, s, pt, i=i: (pt[n, n_pg * s + i], 0))
                  for i in range(n_pg)]
    grid_spec = pltpu.PrefetchScalarGridSpec(
        num_scalar_prefetch=1,
        grid=(n_dec, n_pages // n_pg),
        in_specs=[
            seq_spec, seq_spec, seq_spec, *page_specs, *page_specs,
            pl.BlockSpec((n_pages, rows, page_rows), lambda n, j, pt: (0, 0, 0)),
            pl.BlockSpec((rows, rows), lambda n, j, pt: (0, 0)),
        ],
        out_specs=seq_spec,
        scratch_shapes=[
            pltpu.VMEM((n_full, N_HEADS, HEAD_DIM), F32),
            pltpu.VMEM((n_full, rows, 1), F32),
            pltpu.VMEM((n_full, rows, 1), F32),
            pltpu.VMEM((n_full, rows, HEAD_DIM), F32),
        ])
    ck = cache_k.reshape(-1, HEAD_DIM)
    cv = cache_v.reshape(-1, HEAD_DIM)
    return pl.pallas_call(
        _attn_sample_body, grid_spec=grid_spec,
        out_shape=jax.ShapeDtypeStruct(q.shape, F32),
        compiler_params=_params(2), name="attn_sample")(
            page_table, q, k_new, v_new, *([ck] * n_pg), *([cv] * n_pg), bias_past, bias_own)


def _bias_of_dist(rel_bias, dist):
    onehot = _rel_bucket(dist)[..., None] == jnp.arange(REL_BUCKETS)
    return jnp.sum(jnp.where(onehot[..., None], rel_bias, 0.0), axis=-2)


def _toeplitz(w, n):
    h = w.shape[0]
    wp = jnp.concatenate([w, jnp.zeros((h, 1), w.dtype)], axis=1)
    x = jnp.tile(wp, (1, n))[:, :n * (2 * n - 1)].reshape(h, n, 2 * n - 1)
    return x[:, :, n - 1:]


def _prompt_bias_tiles(rel_bias):
    assert REL_MAX_DIST <= MOBA_BLOCK
    n = MOBA_BLOCK
    tiles = []
    for delta in range(3):
        dist = delta * n - (n - 1) + jnp.arange(2 * n - 1)
        w = jnp.where((dist >= 0)[:, None], _bias_of_dist(rel_bias, dist), -jnp.inf)
        tiles.append(_toeplitz(w.T, n))
    return jnp.stack(tiles, axis=1)


def _sample_bias_tables(rel_bias, past, n_new):
    rows = n_new * N_HEADS
    n_pages = past // PAGE_SIZE
    same_head = jnp.eye(N_HEADS, dtype=bool)
    qpos = past + jnp.arange(n_new)
    dist = qpos[:, None] - jnp.arange(past)[None, :]
    bp = _bias_of_dist(rel_bias, dist)
    bp = bp.reshape(n_new, n_pages, PAGE_SIZE, N_HEADS).transpose(1, 0, 3, 2)
    bp = jnp.where(same_head[None, None, :, None, :], bp[..., None], -jnp.inf)
    bp = bp.reshape(n_pages, rows, PAGE_SIZE * N_HEADS)
    d_own = jnp.arange(n_new)[:, None] - jnp.arange(n_new)[None, :]
    bo = jnp.where((d_own >= 0)[..., None], _bias_of_dist(rel_bias, d_own), -jnp.inf)
    bo = bo.transpose(0, 2, 1)
    bo = jnp.where(same_head[None, :, None, :], bo[..., None], -jnp.inf)
    return bp, bo.reshape(rows, rows)


def _trunk(x, mods, wts, *, mode, rows_per_seq, bm, conv_fn, attn_fn):
    mod0, mod1, modkv = mods
    M = lambda arr, c: _Mod(arr, c, mode, rows_per_seq)
    d_ff = wts["ffn_gate"][0].shape[1]
    bn = 512

    def ffn(xin, layer, mod):
        gu = _fused_mm(xin, [(wts["ffn_gate"][layer], 0), (wts["ffn_up"][layer], 0)], bm=bm, bn=bn,
                       n_cols=d_ff, out_dtypes=[BF16], prologue="rmsmod",
                       pro_args=(wts["norm_ffn_g"][layer], M(mod, 3), M(mod, 4)),
                       epilogue="swiglu", name=f"ffn_up{layer}_{mode}")
        return _fused_mm(gu, [(wts["ffn_down"][layer], 0)], bm=bm, bn=bn, n_cols=D_MODEL,
                         out_dtypes=[F32], epilogue="resid", epi_args=(xin, M(mod, 5)),
                         name=f"ffn_down{layer}_{mode}")

    z = _fused_mm(x, [(wts["conv_in"], 0), (wts["conv_in"], D_MODEL)], bm=bm, bn=bn, n_cols=D_MODEL,
                  out_dtypes=[F32], prologue="rmsmod",
                  pro_args=(wts["norm_mix_g"][0], M(mod0, 0), M(mod0, 1)), epilogue="glu",
                  name=f"conv_in_{mode}")
    yc = conv_fn(z)
    x = _fused_mm(yc, [(wts["conv_out"], 0)], bm=bm, bn=bn, n_cols=D_MODEL, out_dtypes=[F32],
                  prologue="lnsilu", pro_args=(wts["conv_b_dw"], wts["conv_ln_g"], wts["conv_ln_b"]),
                  epilogue="resid", epi_args=(x, M(mod0, 2)), name=f"conv_out_{mode}")
    x = ffn(x, 0, mod0)

    k, k_bf, v, v_bf = _fused_mm(
        x, [(wts["kv_k"], 0), (wts["kv_v"], 0)], bm=bm, bn=bn, n_cols=D_MODEL,
        out_dtypes=[F32, BF16, F32, BF16], prologue="rmsmod",
        pro_args=(wts["norm_kv_g"], M(modkv, 0), M(modkv, 1)), epilogue="kv",
        epi_args=(wts["kv_g_k"],), name=f"kv_{mode}")

    q = _fused_mm(x, [(wts["attn_q"], 0)], bm=bm, bn=bn, n_cols=D_MODEL, out_dtypes=[F32],
                  prologue="rmsmod", pro_args=(wts["norm_mix_g"][1], M(mod1, 0), M(mod1, 1)),
                  epilogue="headnorm", epi_args=(wts["attn_g_q"],), head_scale=HEAD_DIM ** -0.5,
                  name=f"q_{mode}")
    o = attn_fn(q, k, v, k_bf, v_bf)
    x = _fused_mm(o, [(wts["attn_o"], 0)], bm=bm, bn=bn, n_cols=D_MODEL, out_dtypes=[F32],
                  epilogue="resid", epi_args=(x, M(mod1, 2)), name=f"attn_o_{mode}")
    x = ffn(x, 1, mod1)
    return x, k, v, z


def kernel(x_prompt, x_sample, c_prompt, c_sample, cache_k, cache_v, state_conv, page_table, ada_w, ada_b, norm_mix_g, norm_ffn_g, conv_w_in, conv_w_dw, conv_b_dw, conv_ln_g, conv_ln_b, conv_w_out, kv_ada_w, kv_ada_b, norm_kv_g, kv_w_k, kv_w_v, kv_g_k, attn_w_q, attn_g_q, attn_w_o, rel_bias, ffn_w_gate, ffn_w_up, ffn_w_down):
    n_seq, seq, d = x_prompt.shape
    n_dec, n_new, _ = x_sample.shape
    assert d == D_MODEL and ada_w.shape[0] == 2 and state_conv.shape[0] == 1
    past = page_table.shape[1] * PAGE_SIZE
    assert past % MOBA_BLOCK == 0 and seq % MOBA_BLOCK == 0

    n_c = n_seq + n_dec
    c_rows = -(-n_c // 16) * 16
    c_all = jnp.concatenate([c_prompt, c_sample, jnp.zeros((c_rows - n_c, d), F32)], axis=0)

    def ada(w, b, name):
        return _fused_mm(c_all, [(w, 0)], bm=c_rows, bn=512, n_cols=w.shape[1], out_dtypes=[F32],
                         epilogue="bias", epi_args=(b,), name=name)

    mods = [ada(ada_w[0], ada_b[0], "ada0"), ada(ada_w[1], ada_b[1], "ada1"),
            ada(kv_ada_w, kv_ada_b, "ada_kv")]
    mods_p = [m[:n_seq].reshape(n_seq, 1, -1) for m in mods]
    mods_s = [jnp.repeat(m[n_seq:n_c], n_new, axis=0) for m in mods]

    wts = {
        "norm_mix_g": norm_mix_g, "norm_ffn_g": norm_ffn_g,
        "conv_in": conv_w_in[0].astype(BF16), "conv_out": conv_w_out[0].astype(BF16),
        "conv_b_dw": conv_b_dw[0], "conv_ln_g": conv_ln_g[0], "conv_ln_b": conv_ln_b[0],
        "norm_kv_g": norm_kv_g, "kv_k": kv_w_k.astype(BF16), "kv_v": kv_w_v.astype(BF16),
        "kv_g_k": kv_g_k, "attn_q": attn_w_q[0].astype(BF16), "attn_g_q": attn_g_q[0],
        "attn_o": attn_w_o[0].astype(BF16),
        "ffn_gate": ffn_w_gate.astype(BF16), "ffn_up": ffn_w_up.astype(BF16),
        "ffn_down": ffn_w_down.astype(BF16),
    }
    w_dw = conv_w_dw[0]

    bias_tiles = _prompt_bias_tiles(rel_bias)

    def attn_prompt(q, k, v, k_bf, v_bf):
        n_blk = seq // MOBA_BLOCK
        k_mean = _block_means(k, n_seq * n_blk).reshape(n_seq, n_blk, D_MODEL)
        vt_bf = v_bf.reshape(n_seq, n_blk, MOBA_BLOCK, D_MODEL).transpose(0, 1, 3, 2)
        return _attn_prompt(q, k_bf, vt_bf, k_mean, bias_tiles, n_seq, seq)

    y_p, k_p, v_p, z_p = _trunk(
        x_prompt.reshape(n_seq * seq, d), mods_p, wts, mode="seq", rows_per_seq=seq, bm=1024,
        conv_fn=lambda z: _conv_prompt(z, w_dw, n_seq, seq), attn_fn=attn_prompt)

    bias_past, bias_own = _sample_bias_tables(rel_bias, past, n_new)
    new_state = []

    def conv_sample(z):
        y_t, ns = _conv_sample(state_conv[0], z.reshape(n_dec, n_new, d), w_dw)
        new_state.append(ns)
        return y_t.transpose(1, 0, 2).reshape(n_dec * n_new, d)

    def attn_sample(q, k, v, k_bf, v_bf):
        shp = (n_dec, n_new * N_HEADS, HEAD_DIM)
        o = _attn_sample(q.reshape(shp), k.reshape(shp), v.reshape(shp), cache_k, cache_v,
                         page_table, bias_past, bias_own)
        return o.reshape(n_dec * n_new, d)

    y_s, k_s, v_s, z_s = _trunk(
        x_sample.reshape(n_dec * n_new, d), mods_s, wts, mode="row", rows_per_seq=n_new,
        bm=n_dec * n_new, conv_fn=conv_sample, attn_fn=attn_sample)

    heads = (N_HEADS, HEAD_DIM)
    conv_p = z_p.reshape(n_seq, seq, d)[:, seq - (CONV_W - 1):][None]
    conv_s = new_state[0][None]
    return (y_p.reshape(n_seq, seq, d), y_s.reshape(n_dec, n_new, d),
            k_p.reshape(n_seq, seq, *heads), v_p.reshape(n_seq, seq, *heads),
            k_s.reshape(n_dec, n_new, *heads), v_s.reshape(n_dec, n_new, *heads),
            conv_p, conv_s)
```

```python
import functools
import math

import jax
import jax.numpy as jnp
from jax import lax
from jax.experimental import pallas as pl
from jax.experimental.pallas import tpu as pltpu

F32 = jnp.float32
BF16 = jnp.bfloat16

D_MODEL = 2048
HEAD_DIM = 128
N_HEADS = D_MODEL // HEAD_DIM
CONV_W = 31
MOBA_BLOCK = 256
MOBA_TOPK = 3
REL_BUCKETS = 32
REL_MAX_DIST = 128
PAGE_SIZE = 128
NORM_EPS = 1e-6

SUBLANES = 8
CONV_HALO = 32
VMEM_LIMIT_BYTES = 56 * 1024 * 1024

_CONTRACT_LAST = (((1,), (1,)), ((), ()))


def _params(n_axes):
    return pltpu.CompilerParams(dimension_semantics=("arbitrary",) * n_axes,
                                vmem_limit_bytes=VMEM_LIMIT_BYTES)


def _sigmoid(x):
    return 1.0 / (1.0 + jnp.exp(-x))


def _head_rms(a, gain, scale):
    outs = []
    for c in range(a.shape[1] // HEAD_DIM):
        blk = a[:, c * HEAD_DIM:(c + 1) * HEAD_DIM]
        ms = jnp.mean(blk * blk, axis=-1, keepdims=True)
        y = blk * lax.rsqrt(ms + NORM_EPS) * gain
        if scale != 1.0:
            y = y * scale
        outs.append(y)
    return jnp.concatenate(outs, axis=1)


def _fused_mm_body(*refs, prologue, epilogue, n_w, n_out, head_scale, row_chunk):
    refs = list(refs)
    h_ref = refs.pop() if prologue is not None else None
    outs = refs[len(refs) - n_out:]
    it = iter(refs[:len(refs) - n_out])
    x_ref = next(it)
    if prologue == "rmsmod":
        g_ref, sh_ref, sc_ref = next(it), next(it), next(it)
    elif prologue == "lnsilu":
        cb_ref, lg_ref, lb_ref = next(it), next(it), next(it)
    w_refs = [next(it) for _ in range(n_w)]
    if epilogue == "bias":
        b_ref = next(it)
    elif epilogue == "resid":
        res_ref, gate_ref = next(it), next(it)
    elif epilogue in ("headnorm", "kv"):
        hg_ref = next(it)

    bm = x_ref.shape[0]

    if prologue is not None:
        @pl.when(pl.program_id(1) == 0)
        def _prologue():
            def vec(ref, r0):
                return ref[...] if ref.shape[0] == 1 else ref[pl.ds(r0, row_chunk), :]

            def chunk(c, carry):
                r0 = pl.multiple_of(c * row_chunk, row_chunk)
                x = x_ref[pl.ds(r0, row_chunk), :].astype(F32)
                if prologue == "rmsmod":
                    ms = jnp.mean(x * x, axis=-1, keepdims=True)
                    y = x * lax.rsqrt(ms + NORM_EPS) * g_ref[...]
                    h = y * (1.0 + vec(sc_ref, r0)) + vec(sh_ref, r0)
                else:
                    x = x + cb_ref[...]
                    mu = jnp.mean(x, axis=-1, keepdims=True)
                    xc = x - mu
                    y = xc * lax.rsqrt(jnp.mean(xc * xc, axis=-1, keepdims=True) + NORM_EPS)
                    y = y * lg_ref[...] + lb_ref[...]
                    h = y * _sigmoid(y)
                h_ref[pl.ds(r0, row_chunk), :] = h.astype(BF16)
                return carry

            lax.fori_loop(0, bm // row_chunk, chunk, 0)

        h = h_ref[...]
    else:
        h = x_ref[...].astype(BF16)

    accs = [jnp.dot(h, w[...].astype(BF16), preferred_element_type=F32) for w in w_refs]

    if epilogue == "bias":
        res = [accs[0] + b_ref[...]]
    elif epilogue == "glu":
        res = [accs[0] * _sigmoid(accs[1])]
    elif epilogue == "swiglu":
        res = [accs[0] * _sigmoid(accs[0]) * accs[1]]
    elif epilogue == "resid":
        res = [res_ref[...] + gate_ref[...] * accs[0]]
    elif epilogue == "headnorm":
        res = [_head_rms(accs[0], hg_ref[...], head_scale)]
    elif epilogue == "kv":
        k = _head_rms(accs[0], hg_ref[...], head_scale)
        res = [k, k, accs[1], accs[1]]
    else:
        res = [accs[0]]
    if len(res) == 1:
        res = res * n_out
    for o_ref, r in zip(outs, res):
        o_ref[...] = r.astype(o_ref.dtype)


class _Mod:
    def __init__(self, arr, chunk, mode, rows_per_seq):
        self.arr, self.chunk, self.mode, self.rows_per_seq = arr, chunk, mode, rows_per_seq

    def spec(self, bm, width, follow_n):
        per = D_MODEL // width
        base = self.chunk * per
        if self.mode == "seq":
            blocks_per_seq = self.rows_per_seq // bm
            return pl.BlockSpec(
                (None, 1, width),
                lambda i, j: (i // blocks_per_seq, 0, base + (j if follow_n else 0)))
        return pl.BlockSpec((bm, width), lambda i, j: (i, base + (j if follow_n else 0)))


def _fused_mm(x, ws, *, bm, bn, n_cols, out_dtypes, prologue=None, pro_args=(), epilogue=None,
              epi_args=(), head_scale=1.0, name):
    m, k = x.shape
    assert m % bm == 0 and n_cols % bn == 0
    grid = (m // bm, n_cols // bn)
    row_chunk = 128 if bm % 128 == 0 else bm

    in_specs = [pl.BlockSpec((bm, k), lambda i, j: (i, 0))]
    args = [x]
    full_vec = pl.BlockSpec((1, k), lambda i, j: (0, 0))
    if prologue == "rmsmod":
        g, sh, sc = pro_args
        in_specs += [full_vec, sh.spec(bm, k, False), sc.spec(bm, k, False)]
        args += [g.reshape(1, k), sh.arr, sc.arr]
    elif prologue == "lnsilu":
        in_specs += [full_vec] * 3
        args += [a.reshape(1, k) for a in pro_args]
    for w, col0, *layer in ws:
        assert col0 % bn == 0
        off = col0 // bn
        if layer:
            in_specs.append(pl.BlockSpec((None, k, bn), lambda i, j, off=off, l=layer[0]: (l, 0, off + j)))
        else:
            in_specs.append(pl.BlockSpec((k, bn), lambda i, j, off=off: (0, off + j)))
        args.append(w)
    if epilogue == "bias":
        in_specs.append(pl.BlockSpec((1, bn), lambda i, j: (0, j)))
        args.append(epi_args[0].reshape(1, n_cols))
    elif epilogue == "resid":
        res, gate = epi_args
        in_specs += [pl.BlockSpec((bm, bn), lambda i, j: (i, j)), gate.spec(bm, bn, True)]
        args += [res, gate.arr]
    elif epilogue in ("headnorm", "kv"):
        in_specs.append(pl.BlockSpec((1, HEAD_DIM), lambda i, j: (0, 0)))
        args.append(epi_args[0].reshape(1, HEAD_DIM))

    out_shape = [jax.ShapeDtypeStruct((m, n_cols), dt) for dt in out_dtypes]
    out_specs = [pl.BlockSpec((bm, bn), lambda i, j: (i, j)) for _ in out_dtypes]
    scratch = [pltpu.VMEM((bm, k), BF16)] if prologue is not None else []
    body = functools.partial(_fused_mm_body, prologue=prologue, epilogue=epilogue, n_w=len(ws),
                             n_out=len(out_dtypes), head_scale=head_scale, row_chunk=row_chunk)
    out = pl.pallas_call(
        body, grid=grid, in_specs=in_specs, out_specs=out_specs, out_shape=out_shape,
        scratch_shapes=scratch, compiler_params=_params(2), name=name)(*args)
    return out[0] if len(out) == 1 else out


def _conv_prompt_body(z_ref, halo_ref, w_ref, o_ref, ext_ref, sh_ref, *, sub_rows):
    bt = z_ref.shape[0]
    tb = pl.program_id(1)

    @pl.when(tb == 0)
    def _():
        ext_ref[0:CONV_HALO, :] = jnp.zeros((CONV_HALO, ext_ref.shape[1]), F32)

    @pl.when(tb > 0)
    def _():
        ext_ref[0:CONV_HALO, :] = halo_ref[...]

    ext_ref[CONV_HALO:, :] = z_ref[...]
    n_sh = bt + CONV_HALO - SUBLANES
    for r in range(1, SUBLANES):
        sh_ref[r, 0:n_sh, :] = ext_ref[r:r + n_sh, :]
    first = CONV_HALO - (CONV_W - 1)
    for r0 in range(0, bt, sub_rows):
        acc = None
        for j in range(CONV_W):
            a, r = divmod(first + j, SUBLANES)
            lo = a * SUBLANES + r0
            rows = ext_ref[lo:lo + sub_rows, :] if r == 0 else sh_ref[r, lo:lo + sub_rows, :]
            term = w_ref[j:j + 1, :] * rows
            acc = term if acc is None else acc + term
        o_ref[r0:r0 + sub_rows, :] = acc


def _conv_prompt(z, w_dw, n_seq, seq):
    bt, bd = 512, 256
    t_blocks = seq // bt
    halo_per_seq = seq // CONV_HALO
    halo_per_blk = bt // CONV_HALO
    grid = (n_seq, t_blocks, D_MODEL // bd)
    return pl.pallas_call(
        functools.partial(_conv_prompt_body, sub_rows=32),
        grid=grid,
        in_specs=[
            pl.BlockSpec((bt, bd), lambda n, t, d: (n * t_blocks + t, d)),
            pl.BlockSpec((CONV_HALO, bd),
                         lambda n, t, d: (jnp.maximum(n * halo_per_seq + t * halo_per_blk - 1, 0), d)),
            pl.BlockSpec((CONV_W, bd), lambda n, t, d: (0, d)),
        ],
        out_specs=pl.BlockSpec((bt, bd), lambda n, t, d: (n * t_blocks + t, d)),
        out_shape=jax.ShapeDtypeStruct(z.shape, F32),
        scratch_shapes=[pltpu.VMEM((CONV_HALO + bt, bd), F32),
                        pltpu.VMEM((SUBLANES, CONV_HALO + bt, bd), F32)],
        compiler_params=_params(3), name="conv_prompt")(z, z, w_dw)


def _conv_sample_body(st_ref, z_ref, wt_ref, wz_ref, y_ref, ns_ref):
    st = st_ref[...]
    z = z_ref[...]
    n_old, n_new = st.shape[1], z.shape[1]
    for t in range(n_new):
        y_ref[t] = jnp.sum(st * wt_ref[t][None], axis=1) + jnp.sum(z * wz_ref[t][None], axis=1)
    ns_ref[:, 0:n_old - n_new, :] = st[:, n_new:, :]
    ns_ref[:, n_old - n_new:, :] = z


def _conv_sample(st, z, w_dw):
    _, n_seq, n_old, _ = st.shape
    n_new = z.shape[1]
    bd = 256
    wt = jnp.stack([jnp.pad(w_dw[:n_old - t], ((t, 0), (0, 0))) for t in range(n_new)])
    wz = jnp.stack([jnp.pad(w_dw[n_old - t:], ((0, n_new - 1 - t), (0, 0))) for t in range(n_new)])
    state = pl.BlockSpec((None, n_seq, n_old, bd), lambda d: (0, 0, 0, d))
    taps = lambda rows: pl.BlockSpec((n_new, rows, bd), lambda d: (0, 0, d))
    return pl.pallas_call(
        _conv_sample_body,
        grid=(D_MODEL // bd,),
        in_specs=[state, pl.BlockSpec((n_seq, n_new, bd), lambda d: (0, 0, d)), taps(n_old), taps(n_new)],
        out_specs=[pl.BlockSpec((n_new, n_seq, bd), lambda d: (0, 0, d)), state],
        out_shape=[jax.ShapeDtypeStruct((n_new, n_seq, D_MODEL), F32),
                   jax.ShapeDtypeStruct(st.shape, F32)],
        compiler_params=_params(1), name="conv_sample")(st, z, wt, wz)


def _rel_bucket(dist):
    n = jnp.maximum(dist, 0)
    max_exact = REL_BUCKETS // 2
    nf = jnp.maximum(n, 1).astype(F32)
    large = max_exact + (jnp.log(nf / max_exact) / math.log(REL_MAX_DIST / max_exact)
                         * (REL_BUCKETS - max_exact)).astype(jnp.int32)
    large = jnp.minimum(large, REL_BUCKETS - 1)
    return jnp.where(n < max_exact, n, large)


def _top_blocks_rows(gate, blk_ix, topk):
    cnt = jnp.zeros(gate.shape, jnp.int32)
    for b in range(gate.shape[0]):
        other = gate[b:b + 1, :]
        beats = (other > gate) | ((other == gate) & (b < blk_ix))
        cnt = cnt + beats.astype(jnp.int32)
    return cnt < topk


def _kmean_body(k_ref, o_ref):
    o_ref[...] = jnp.sum(k_ref[...], axis=0, keepdims=True) / MOBA_BLOCK


def _block_means(k, n_blocks):
    return pl.pallas_call(
        _kmean_body, grid=(n_blocks,),
        in_specs=[pl.BlockSpec((MOBA_BLOCK, D_MODEL), lambda b: (b, 0))],
        out_specs=pl.BlockSpec((None, 1, D_MODEL), lambda b: (b, 0, 0)),
        out_shape=jax.ShapeDtypeStruct((n_blocks, 1, D_MODEL), F32),
        compiler_params=_params(1), name="kmean_prompt")(k)


ATTN_HEADS_PER_STEP = 8


def _attn_prompt_body(q_ref, k_ref, vt_ref, km_ref, bias_ref, o_ref, sel_ref):
    blk = MOBA_BLOCK
    hd = HEAD_DIM
    n_blk = km_ref.shape[0]
    n_heads = q_ref.shape[1] // hd
    qb = pl.program_id(2)
    r_own = pl.multiple_of(qb * blk, blk)
    blk_ix = lax.broadcasted_iota(jnp.int32, (n_blk, n_heads * blk), 0)
    valid = blk_ix < qb

    heads = range(n_heads)
    cols = [slice(g * hd, (g + 1) * hd) for g in heads]
    q = [q_ref[:, cols[g]] for g in heads]
    qh = [q[g].astype(BF16) for g in heads]
    s_t = [lax.dot_general(k_ref[pl.ds(r_own, blk), cols[g]], qh[g], _CONTRACT_LAST,
                           preferred_element_type=F32) + bias_ref[g, 0] for g in heads]
    gate = [lax.dot_general(km_ref[:, cols[g]], q[g], _CONTRACT_LAST, precision=lax.Precision.HIGHEST,
                            preferred_element_type=F32) for g in heads]
    gate = jnp.where(valid, jnp.concatenate(gate, axis=1), -jnp.inf)
    sel_ref[...] = (valid & _top_blocks_rows(gate, blk_ix, MOBA_TOPK)).astype(F32)
    m = [jnp.max(s_t[g], axis=0, keepdims=True) for g in heads]
    p = [jnp.exp(s_t[g] - m[g]) for g in heads]
    l = [jnp.sum(p[g], axis=0, keepdims=True) for g in heads]
    acc = [jnp.dot(vt_ref[qb, cols[g], :], p[g].astype(BF16), preferred_element_type=F32)
           for g in heads]
    state = [(m[g], l[g], acc[g]) for g in heads]

    def body(b, state):
        r0 = pl.multiple_of(b * blk, blk)
        tile = jnp.minimum(qb - b, 2)
        s_t = [lax.dot_general(k_ref[pl.ds(r0, blk), cols[g]], qh[g], _CONTRACT_LAST,
                               preferred_element_type=F32) for g in heads]
        s_t = [jnp.where(sel_ref[pl.ds(b, 1), g * blk:(g + 1) * blk] > 0.0,
                         s_t[g] + bias_ref[g, tile], -jnp.inf) for g in heads]
        m_new = [jnp.maximum(state[g][0], jnp.max(s_t[g], axis=0, keepdims=True)) for g in heads]
        alpha = [jnp.exp(state[g][0] - m_new[g]) for g in heads]
        p = [jnp.exp(s_t[g] - m_new[g]) for g in heads]
        l = [alpha[g] * state[g][1] + jnp.sum(p[g], axis=0, keepdims=True) for g in heads]
        pv = [jnp.dot(vt_ref[b, cols[g], :], p[g].astype(BF16), preferred_element_type=F32)
              for g in heads]
        return [(m_new[g], l[g], alpha[g] * state[g][2] + pv[g]) for g in heads]

    state = lax.fori_loop(0, qb, body, state)
    for g in range(n_heads):
        m, l, acc = state[g]
        o_ref[:, g * hd:(g + 1) * hd] = (acc / l).T.astype(o_ref.dtype)


def _attn_prompt(q, k_bf, vt_bf, k_mean, bias_tiles, n_seq, seq):
    n_blk = seq // MOBA_BLOCK
    hps = ATTN_HEADS_PER_STEP
    w = hps * HEAD_DIM
    grid = (n_seq, N_HEADS // hps, n_blk)
    return pl.pallas_call(
        _attn_prompt_body, grid=grid,
        in_specs=[
            pl.BlockSpec((MOBA_BLOCK, w), lambda n, h, b: (n * n_blk + b, h)),
            pl.BlockSpec((seq, w), lambda n, h, b: (n, h)),
            pl.BlockSpec((None, n_blk, w, MOBA_BLOCK), lambda n, h, b: (n, 0, h, 0)),
            pl.BlockSpec((None, n_blk, w), lambda n, h, b: (n, 0, h)),
            pl.BlockSpec((hps, 3, MOBA_BLOCK, MOBA_BLOCK), lambda n, h, b: (h, 0, 0, 0)),
        ],
        out_specs=pl.BlockSpec((MOBA_BLOCK, w), lambda n, h, b: (n * n_blk + b, h)),
        out_shape=jax.ShapeDtypeStruct(q.shape, BF16),
        scratch_shapes=[pltpu.VMEM((n_blk, hps * MOBA_BLOCK), F32)],
        compiler_params=_params(3), name="attn_prompt")(q, k_bf, vt_bf, k_mean, bias_tiles)


SAMPLE_BLOCKS_PER_STEP = 2
PAGES_PER_BLOCK = MOBA_BLOCK // PAGE_SIZE


SAMPLE_RING = 3


def _attn_sample_body(pt_ref, q_ref, kn_ref, vn_ref, ck_hbm, cv_hbm, bp_ref, bo_ref, o_ref,
                      kbuf, vbuf, sem, kblk_ref, m_ref, l_ref, acc_ref, *, n_seq, n_steps):
    n_pg = SAMPLE_BLOCKS_PER_STEP * PAGES_PER_BLOCK
    page_rows = PAGE_SIZE * N_HEADS
    rows = q_ref.shape[0]
    n_new = rows // N_HEADS
    n_full = kblk_ref.shape[0]
    step = pl.program_id(1)
    last = n_steps - 1
    g = pl.program_id(0) * n_steps + step
    total = n_seq * n_steps

    def page_copies(gg):
        slot = gg % SAMPLE_RING
        seq, first = gg // n_steps, (gg % n_steps) * n_pg
        copies = []
        for i in range(n_pg):
            row0 = pl.multiple_of(pt_ref[seq, first + i] * page_rows, page_rows)
            for hbm, buf in ((ck_hbm, kbuf), (cv_hbm, vbuf)):
                copies.append(pltpu.make_async_copy(hbm.at[pl.ds(row0, page_rows)], buf.at[slot, i],
                                                    sem.at[slot]))
        return copies

    @pl.when(g == 0)
    def _prime():
        for gg in range(min(SAMPLE_RING - 1, total)):
            for c in page_copies(gg):
                c.start()

    @pl.when(g + SAMPLE_RING - 1 < total)
    def _prefetch():
        for c in page_copies(g + SAMPLE_RING - 1):
            c.start()

    for c in page_copies(g):
        c.wait()
    slot = g % SAMPLE_RING

    qh = q_ref[...].astype(BF16)
    pages = range(n_pg)
    kp = [kbuf[slot, i] for i in pages]
    s = [lax.dot_general(qh, kp[i].astype(BF16), _CONTRACT_LAST, preferred_element_type=F32)
         + bp_ref[n_pg * step + i] for i in pages]
    page_sum = [jnp.sum(kp[i].reshape(PAGE_SIZE, N_HEADS, HEAD_DIM), axis=0) for i in pages]
    page_max = [jnp.max(s[i], axis=-1, keepdims=True) for i in pages]
    m_blk = [functools.reduce(jnp.maximum, page_max[c * PAGES_PER_BLOCK:(c + 1) * PAGES_PER_BLOCK])
             for c in range(SAMPLE_BLOCKS_PER_STEP)]
    p = [jnp.exp(s[i] - m_blk[i // PAGES_PER_BLOCK]) for i in pages]
    p_sum = [jnp.sum(p[i], axis=-1, keepdims=True) for i in pages]
    pv = [jnp.dot(p[i].astype(BF16), vbuf[slot, i].astype(BF16), preferred_element_type=F32)
          for i in pages]
    for c in range(SAMPLE_BLOCKS_PER_STEP):
        b = SAMPLE_BLOCKS_PER_STEP * step + c
        of_blk = slice(c * PAGES_PER_BLOCK, (c + 1) * PAGES_PER_BLOCK)
        kblk_ref[b] = functools.reduce(jnp.add, page_sum[of_blk])
        m_ref[b] = m_blk[c]
        l_ref[b] = functools.reduce(jnp.add, p_sum[of_blk])
        acc_ref[b] = functools.reduce(jnp.add, pv[of_blk])

    @pl.when(step == last)
    def _combine():
        q = q_ref[...]
        gates = []
        for c in range(n_full):
            k_mean = kblk_ref[c] / MOBA_BLOCK
            k_rows = jnp.concatenate([k_mean] * n_new, axis=0)
            gates.append(jnp.sum(q * k_rows, axis=-1, keepdims=True))
        sel = []
        for c in range(n_full):
            cnt = jnp.zeros((rows, 1), jnp.int32)
            for o in range(n_full):
                beats = gates[o] > gates[c]
                if o < c:
                    beats = beats | (gates[o] == gates[c])
                cnt = cnt + beats.astype(jnp.int32)
            sel.append(cnt < MOBA_TOPK)

        s_own = lax.dot_general(q, kn_ref[...], _CONTRACT_LAST, precision=lax.Precision.HIGHEST,
                                preferred_element_type=F32) + bo_ref[...]
        m = jnp.max(s_own, axis=-1, keepdims=True)
        for c in range(n_full):
            m = jnp.maximum(m, jnp.where(sel[c], m_ref[c], -jnp.inf))
        p_own = jnp.exp(s_own - m)
        l = jnp.sum(p_own, axis=-1, keepdims=True)
        acc = jnp.dot(p_own, vn_ref[...], precision=lax.Precision.HIGHEST,
                      preferred_element_type=F32)
        for c in range(n_full):
            w = jnp.where(sel[c], jnp.exp(m_ref[c] - m), 0.0)
            l = l + w * l_ref[c]
            acc = acc + w * acc_ref[c]
        o_ref[...] = acc / l


def _attn_sample(q, k_new, v_new, cache_k, cache_v, page_table, bias_past, bias_own):
    n_dec, rows, _ = q.shape
    n_pages = page_table.shape[1]
    n_full = n_pages * PAGE_SIZE // MOBA_BLOCK
    n_pg = SAMPLE_BLOCKS_PER_STEP * PAGES_PER_BLOCK
    assert n_pages % n_pg == 0
    n_steps = n_pages // n_pg
    page_rows = PAGE_SIZE * N_HEADS
    seq_spec = pl.BlockSpec((None, rows, HEAD_DIM), lambda n, s, pt: (n, 0, 0))
    cache_spec = pl.BlockSpec(memory_space=pl.ANY)
    grid_spec = pltpu.PrefetchScalarGridSpec(
        num_scalar_prefetch=1,
        grid=(n_dec, n_steps),
        in_specs=[
            seq_spec, seq_spec, seq_spec, cache_spec, cache_spec,
            pl.BlockSpec((n_pages, rows, page_rows), lambda n, j, pt: (0, 0, 0)),
            pl.BlockSpec((rows, rows), lambda n, j, pt: (0, 0)),
        ],
        out_specs=seq_spec,
        scratch_shapes=[
            pltpu.VMEM((SAMPLE_RING, n_pg, page_rows, HEAD_DIM), F32),
            pltpu.VMEM((SAMPLE_RING, n_pg, page_rows, HEAD_DIM), F32),
            pltpu.SemaphoreType.DMA((SAMPLE_RING,)),
            pltpu.VMEM((n_full, N_HEADS, HEAD_DIM), F32),
            pltpu.VMEM((n_full, rows, 1), F32),
            pltpu.VMEM((n_full, rows, 1), F32),
            pltpu.VMEM((n_full, rows, HEAD_DIM), F32),
        ])
    ck = cache_k.reshape(-1, HEAD_DIM)
    cv = cache_v.reshape(-1, HEAD_DIM)
    return pl.pallas_call(
        functools.partial(_attn_sample_body, n_seq=n_dec, n_steps=n_steps), grid_spec=grid_spec,
        out_shape=jax.ShapeDtypeStruct(q.shape, F32),
        compiler_params=_params(2), name="attn_sample")(
            page_table, q, k_new, v_new, ck, cv, bias_past, bias_own)


def _bias_of_dist(rel_bias, dist):
    onehot = _rel_bucket(dist)[..., None] == jnp.arange(REL_BUCKETS)
    return jnp.sum(jnp.where(onehot[..., None], rel_bias, 0.0), axis=-2)


def _toeplitz(w, n):
    h = w.shape[0]
    wp = jnp.concatenate([w, jnp.zeros((h, 1), w.dtype)], axis=1)
    x = jnp.tile(wp, (1, n))[:, :n * (2 * n - 1)].reshape(h, n, 2 * n - 1)
    return x[:, :, n - 1:]


def _prompt_bias_tiles(rel_bias):
    assert REL_MAX_DIST <= MOBA_BLOCK
    n = MOBA_BLOCK
    tiles = []
    for delta in range(3):
        dist = delta * n - (n - 1) + jnp.arange(2 * n - 1)
        w = jnp.where((dist >= 0)[:, None], _bias_of_dist(rel_bias, dist), -jnp.inf)
        tiles.append(_toeplitz(w.T, n))
    return jnp.stack(tiles, axis=1)


def _sample_bias_tables(rel_bias, past, n_new):
    rows = n_new * N_HEADS
    n_pages = past // PAGE_SIZE
    same_head = jnp.eye(N_HEADS, dtype=bool)
    qpos = past + jnp.arange(n_new)
    dist = qpos[:, None] - jnp.arange(past)[None, :]
    bp = _bias_of_dist(rel_bias, dist)
    bp = bp.reshape(n_new, n_pages, PAGE_SIZE, N_HEADS).transpose(1, 0, 3, 2)
    col = jnp.arange(PAGE_SIZE * N_HEADS)
    spread = (col[None, :] // N_HEADS == jnp.arange(PAGE_SIZE)[:, None]).astype(F32)
    bp = jnp.einsum("prk,kc->prc", bp.reshape(n_pages, rows, PAGE_SIZE), spread,
                    precision=lax.Precision.HIGHEST)
    head_ok = (jnp.arange(rows) % N_HEADS)[:, None] == (col % N_HEADS)[None, :]
    bp = jnp.where(head_ok[None], bp, -jnp.inf)
    d_own = jnp.arange(n_new)[:, None] - jnp.arange(n_new)[None, :]
    bo = jnp.where((d_own >= 0)[..., None], _bias_of_dist(rel_bias, d_own), -jnp.inf)
    bo = bo.transpose(0, 2, 1)
    bo = jnp.where(same_head[None, :, None, :], bo[..., None], -jnp.inf)
    return bp, bo.reshape(rows, rows)


def _trunk(x, mods, wts, *, mode, rows_per_seq, bm, conv_fn, attn_fn):
    mod0, mod1, modkv = mods
    M = lambda arr, c: _Mod(arr, c, mode, rows_per_seq)
    d_ff = wts["ffn_gate"][0].shape[1]
    bn = 512

    def ffn(xin, layer, mod):
        gu = _fused_mm(xin, [(wts["ffn_gate"][layer], 0), (wts["ffn_up"][layer], 0)], bm=bm, bn=bn,
                       n_cols=d_ff, out_dtypes=[BF16], prologue="rmsmod",
                       pro_args=(wts["norm_ffn_g"][layer], M(mod, 3), M(mod, 4)),
                       epilogue="swiglu", name=f"ffn_up{layer}_{mode}")
        return _fused_mm(gu, [(wts["ffn_down"][layer], 0)], bm=bm, bn=bn, n_cols=D_MODEL,
                         out_dtypes=[F32], epilogue="resid", epi_args=(xin, M(mod, 5)),
                         name=f"ffn_down{layer}_{mode}")

    z = _fused_mm(x, [(wts["conv_in"], 0), (wts["conv_in"], D_MODEL)], bm=bm, bn=bn, n_cols=D_MODEL,
                  out_dtypes=[F32], prologue="rmsmod",
                  pro_args=(wts["norm_mix_g"][0], M(mod0, 0), M(mod0, 1)), epilogue="glu",
                  name=f"conv_in_{mode}")
    yc = conv_fn(z)
    x = _fused_mm(yc, [(wts["conv_out"], 0)], bm=bm, bn=bn, n_cols=D_MODEL, out_dtypes=[F32],
                  prologue="lnsilu", pro_args=(wts["conv_b_dw"], wts["conv_ln_g"], wts["conv_ln_b"]),
                  epilogue="resid", epi_args=(x, M(mod0, 2)), name=f"conv_out_{mode}")
    x = ffn(x, 0, mod0)

    k, k_bf, v, v_bf = _fused_mm(
        x, [(wts["kv_k"], 0), (wts["kv_v"], 0)], bm=bm, bn=bn, n_cols=D_MODEL,
        out_dtypes=[F32, BF16, F32, BF16], prologue="rmsmod",
        pro_args=(wts["norm_kv_g"], M(modkv, 0), M(modkv, 1)), epilogue="kv",
        epi_args=(wts["kv_g_k"],), name=f"kv_{mode}")

    q = _fused_mm(x, [(wts["attn_q"], 0)], bm=bm, bn=bn, n_cols=D_MODEL, out_dtypes=[F32],
                  prologue="rmsmod", pro_args=(wts["norm_mix_g"][1], M(mod1, 0), M(mod1, 1)),
                  epilogue="headnorm", epi_args=(wts["attn_g_q"],), head_scale=HEAD_DIM ** -0.5,
                  name=f"q_{mode}")
    o = attn_fn(q, k, v, k_bf, v_bf)
    x = _fused_mm(o, [(wts["attn_o"], 0)], bm=bm, bn=bn, n_cols=D_MODEL, out_dtypes=[F32],
                  epilogue="resid", epi_args=(x, M(mod1, 2)), name=f"attn_o_{mode}")
    x = ffn(x, 1, mod1)
    return x, k, v, z


def kernel(x_prompt, x_sample, c_prompt, c_sample, cache_k, cache_v, state_conv, page_table, ada_w, ada_b, norm_mix_g, norm_ffn_g, conv_w_in, conv_w_dw, conv_b_dw, conv_ln_g, conv_ln_b, conv_w_out, kv_ada_w, kv_ada_b, norm_kv_g, kv_w_k, kv_w_v, kv_g_k, attn_w_q, attn_g_q, attn_w_o, rel_bias, ffn_w_gate, ffn_w_up, ffn_w_down):
    n_seq, seq, d = x_prompt.shape
    n_dec, n_new, _ = x_sample.shape
    assert d == D_MODEL and ada_w.shape[0] == 2 and state_conv.shape[0] == 1
    past = page_table.shape[1] * PAGE_SIZE
    assert past % MOBA_BLOCK == 0 and seq % MOBA_BLOCK == 0

    n_c = n_seq + n_dec
    c_rows = -(-n_c // 16) * 16
    c_all = jnp.concatenate([c_prompt, c_sample, jnp.zeros((c_rows - n_c, d), F32)], axis=0)

    def ada(w, b, name):
        return _fused_mm(c_all, [w], bm=c_rows, bn=512, n_cols=w[0].shape[-1], out_dtypes=[F32],
                         epilogue="bias", epi_args=(b,), name=name)

    mods = [ada((ada_w, 0, 0), ada_b[0], "ada0"), ada((ada_w, 0, 1), ada_b[1], "ada1"),
            ada((kv_ada_w, 0), kv_ada_b, "ada_kv")]
    mods_p = [m[:n_seq].reshape(n_seq, 1, -1) for m in mods]
    mods_s = [jnp.repeat(m[n_seq:n_c], n_new, axis=0) for m in mods]

    wts = {
        "norm_mix_g": norm_mix_g, "norm_ffn_g": norm_ffn_g,
        "conv_in": conv_w_in[0].astype(BF16), "conv_out": conv_w_out[0].astype(BF16),
        "conv_b_dw": conv_b_dw[0], "conv_ln_g": conv_ln_g[0], "conv_ln_b": conv_ln_b[0],
        "norm_kv_g": norm_kv_g, "kv_k": kv_w_k.astype(BF16), "kv_v": kv_w_v.astype(BF16),
        "kv_g_k": kv_g_k, "attn_q": attn_w_q[0].astype(BF16), "attn_g_q": attn_g_q[0],
        "attn_o": attn_w_o[0].astype(BF16),
        "ffn_gate": [w.astype(BF16) for w in ffn_w_gate], "ffn_up": [w.astype(BF16) for w in ffn_w_up],
        "ffn_down": [w.astype(BF16) for w in ffn_w_down],
    }
    w_dw = conv_w_dw[0]

    bias_tiles = _prompt_bias_tiles(rel_bias)

    def attn_prompt(q, k, v, k_bf, v_bf):
        n_blk = seq // MOBA_BLOCK
        k_mean = _block_means(k, n_seq * n_blk).reshape(n_seq, n_blk, D_MODEL)
        vt_bf = v_bf.reshape(n_seq, n_blk, MOBA_BLOCK, D_MODEL).transpose(0, 1, 3, 2)
        return _attn_prompt(q, k_bf, vt_bf, k_mean, bias_tiles, n_seq, seq)

    y_p, k_p, v_p, z_p = _trunk(
        x_prompt.reshape(n_seq * seq, d), mods_p, wts, mode="seq", rows_per_seq=seq, bm=1024,
        conv_fn=lambda z: _conv_prompt(z, w_dw, n_seq, seq), attn_fn=attn_prompt)

    bias_past, bias_own = _sample_bias_tables(rel_bias, past, n_new)
    new_state = []

    def conv_sample(z):
        y_t, ns = _conv_sample(state_conv, z.reshape(n_dec, n_new, d), w_dw)
        new_state.append(ns)
        return y_t.transpose(1, 0, 2).reshape(n_dec * n_new, d)

    def attn_sample(q, k, v, k_bf, v_bf):
        shp = (n_dec, n_new * N_HEADS, HEAD_DIM)
        o = _attn_sample(q.reshape(shp), k.reshape(shp), v.reshape(shp), cache_k, cache_v,
                         page_table, bias_past, bias_own)
        return o.reshape(n_dec * n_new, d)

    y_s, k_s, v_s, z_s = _trunk(
        x_sample.reshape(n_dec * n_new, d), mods_s, wts, mode="row", rows_per_seq=n_new,
        bm=n_dec * n_new, conv_fn=conv_sample, attn_fn=attn_sample)

    heads = (N_HEADS, HEAD_DIM)
    conv_p = z_p.reshape(n_seq, seq, d)[:, seq - (CONV_W - 1):][None]
    conv_s = new_state[0]
    return (y_p.reshape(n_seq, seq, d), y_s.reshape(n_dec, n_new, d),
            k_p.reshape(n_seq, seq, *heads), v_p.reshape(n_seq, seq, *heads),
            k_s.reshape(n_dec, n_new, *heads), v_s.reshape(n_dec, n_new, *heads),
            conv_p, conv_s)
```

```python
import functools
import math

import jax
import jax.numpy as jnp
from jax import lax
from jax.experimental import pallas as pl
from jax.experimental.pallas import tpu as pltpu

F32 = jnp.float32
BF16 = jnp.bfloat16

D_MODEL = 2048
HEAD_DIM = 128
N_HEADS = D_MODEL // HEAD_DIM
CONV_W = 31
MOBA_BLOCK = 256
MOBA_TOPK = 3
REL_BUCKETS = 32
REL_MAX_DIST = 128
PAGE_SIZE = 128
NORM_EPS = 1e-6

SUBLANES = 8
CONV_HALO = 32
VMEM_LIMIT_BYTES = 56 * 1024 * 1024

_CONTRACT_LAST = (((1,), (1,)), ((), ()))


def _params(n_axes):
    return pltpu.CompilerParams(dimension_semantics=("arbitrary",) * n_axes,
                                vmem_limit_bytes=VMEM_LIMIT_BYTES)


def _sigmoid(x):
    return 1.0 / (1.0 + jnp.exp(-x))


def _head_rms(a, gain, scale):
    outs = []
    for c in range(a.shape[1] // HEAD_DIM):
        blk = a[:, c * HEAD_DIM:(c + 1) * HEAD_DIM]
        ms = jnp.mean(blk * blk, axis=-1, keepdims=True)
        y = blk * lax.rsqrt(ms + NORM_EPS) * gain
        if scale != 1.0:
            y = y * scale
        outs.append(y)
    return jnp.concatenate(outs, axis=1)


def _fused_mm_body(*refs, prologue, epilogue, n_w, n_out, head_scale, row_chunk):
    refs = list(refs)
    h_ref = refs.pop() if prologue is not None else None
    outs = refs[len(refs) - n_out:]
    it = iter(refs[:len(refs) - n_out])
    x_ref = next(it)
    if prologue == "rmsmod":
        g_ref, sh_ref, sc_ref = next(it), next(it), next(it)
    elif prologue == "lnsilu":
        cb_ref, lg_ref, lb_ref = next(it), next(it), next(it)
    w_refs = [next(it) for _ in range(n_w)]
    if epilogue == "bias":
        b_ref = next(it)
    elif epilogue == "resid":
        res_ref, gate_ref = next(it), next(it)
    elif epilogue in ("headnorm", "kv"):
        hg_ref = next(it)

    bm = x_ref.shape[0]

    if prologue is not None:
        @pl.when(pl.program_id(1) == 0)
        def _prologue():
            def vec(ref, r0):
                return ref[...] if ref.shape[0] == 1 else ref[pl.ds(r0, row_chunk), :]

            def chunk(c, carry):
                r0 = pl.multiple_of(c * row_chunk, row_chunk)
                x = x_ref[pl.ds(r0, row_chunk), :].astype(F32)
                if prologue == "rmsmod":
                    ms = jnp.mean(x * x, axis=-1, keepdims=True)
                    y = x * lax.rsqrt(ms + NORM_EPS) * g_ref[...]
                    h = y * (1.0 + vec(sc_ref, r0)) + vec(sh_ref, r0)
                else:
                    x = x + cb_ref[...]
                    mu = jnp.mean(x, axis=-1, keepdims=True)
                    xc = x - mu
                    y = xc * lax.rsqrt(jnp.mean(xc * xc, axis=-1, keepdims=True) + NORM_EPS)
                    y = y * lg_ref[...] + lb_ref[...]
                    h = y * _sigmoid(y)
                h_ref[pl.ds(r0, row_chunk), :] = h.astype(BF16)
                return carry

            lax.fori_loop(0, bm // row_chunk, chunk, 0)

        h = h_ref[...]
    else:
        h = x_ref[...].astype(BF16)

    accs = [jnp.dot(h, w[...].astype(BF16), preferred_element_type=F32) for w in w_refs]

    if epilogue == "bias":
        res = [accs[0] + b_ref[...]]
    elif epilogue == "glu":
        res = [accs[0] * _sigmoid(accs[1])]
    elif epilogue == "swiglu":
        res = [accs[0] * _sigmoid(accs[0]) * accs[1]]
    elif epilogue == "resid":
        res = [res_ref[...] + gate_ref[...] * accs[0]]
    elif epilogue == "headnorm":
        res = [_head_rms(accs[0], hg_ref[...], head_scale)]
    elif epilogue == "kv":
        k = _head_rms(accs[0], hg_ref[...], head_scale)
        res = [k, k, accs[1], accs[1]]
    else:
        res = [accs[0]]
    if len(res) == 1:
        res = res * n_out
    for o_ref, r in zip(outs, res):
        o_ref[...] = r.astype(o_ref.dtype)


class _Mod:
    def __init__(self, arr, chunk, mode, rows_per_seq):
        self.arr, self.chunk, self.mode, self.rows_per_seq = arr, chunk, mode, rows_per_seq

    def spec(self, bm, width, follow_n):
        per = D_MODEL // width
        base = self.chunk * per
        if self.mode == "seq":
            blocks_per_seq = self.rows_per_seq // bm
            return pl.BlockSpec(
                (None, 1, width),
                lambda i, j: (i // blocks_per_seq, 0, base + (j if follow_n else 0)))
        return pl.BlockSpec((bm, width), lambda i, j: (i, base + (j if follow_n else 0)))


def _fused_mm(x, ws, *, bm, bn, n_cols, out_dtypes, prologue=None, pro_args=(), epilogue=None,
              epi_args=(), head_scale=1.0, name):
    m, k = x.shape
    assert m % bm == 0 and n_cols % bn == 0
    grid = (m // bm, n_cols // bn)
    row_chunk = 128 if bm % 128 == 0 else bm

    in_specs = [pl.BlockSpec((bm, k), lambda i, j: (i, 0))]
    args = [x]
    full_vec = pl.BlockSpec((1, k), lambda i, j: (0, 0))
    if prologue == "rmsmod":
        g, sh, sc = pro_args
        in_specs += [full_vec, sh.spec(bm, k, False), sc.spec(bm, k, False)]
        args += [g.reshape(1, k), sh.arr, sc.arr]
    elif prologue == "lnsilu":
        in_specs += [full_vec] * 3
        args += [a.reshape(1, k) for a in pro_args]
    for w, col0, *layer in ws:
        assert col0 % bn == 0
        off = col0 // bn
        if layer:
            in_specs.append(pl.BlockSpec((None, k, bn), lambda i, j, off=off, l=layer[0]: (l, 0, off + j)))
        else:
            in_specs.append(pl.BlockSpec((k, bn), lambda i, j, off=off: (0, off + j)))
        args.append(w)
    if epilogue == "bias":
        in_specs.append(pl.BlockSpec((1, bn), lambda i, j: (0, j)))
        args.append(epi_args[0].reshape(1, n_cols))
    elif epilogue == "resid":
        res, gate = epi_args
        in_specs += [pl.BlockSpec((bm, bn), lambda i, j: (i, j)), gate.spec(bm, bn, True)]
        args += [res, gate.arr]
    elif epilogue in ("headnorm", "kv"):
        in_specs.append(pl.BlockSpec((1, HEAD_DIM), lambda i, j: (0, 0)))
        args.append(epi_args[0].reshape(1, HEAD_DIM))

    out_shape = [jax.ShapeDtypeStruct((m, n_cols), dt) for dt in out_dtypes]
    out_specs = [pl.BlockSpec((bm, bn), lambda i, j: (i, j)) for _ in out_dtypes]
    scratch = [pltpu.VMEM((bm, k), BF16)] if prologue is not None else []
    body = functools.partial(_fused_mm_body, prologue=prologue, epilogue=epilogue, n_w=len(ws),
                             n_out=len(out_dtypes), head_scale=head_scale, row_chunk=row_chunk)
    out = pl.pallas_call(
        body, grid=grid, in_specs=in_specs, out_specs=out_specs, out_shape=out_shape,
        scratch_shapes=scratch, compiler_params=_params(2), name=name)(*args)
    return out[0] if len(out) == 1 else out


def _conv_prompt_body(z_ref, halo_ref, w_ref, o_ref, ext_ref, sh_ref, *, sub_rows):
    bt = z_ref.shape[0]
    tb = pl.program_id(1)

    @pl.when(tb == 0)
    def _():
        ext_ref[0:CONV_HALO, :] = jnp.zeros((CONV_HALO, ext_ref.shape[1]), F32)

    @pl.when(tb > 0)
    def _():
        ext_ref[0:CONV_HALO, :] = halo_ref[...]

    ext_ref[CONV_HALO:, :] = z_ref[...]
    n_sh = bt + CONV_HALO - SUBLANES
    for r in range(1, SUBLANES):
        sh_ref[r, 0:n_sh, :] = ext_ref[r:r + n_sh, :]
    first = CONV_HALO - (CONV_W - 1)
    for r0 in range(0, bt, sub_rows):
        acc = None
        for j in range(CONV_W):
            a, r = divmod(first + j, SUBLANES)
            lo = a * SUBLANES + r0
            rows = ext_ref[lo:lo + sub_rows, :] if r == 0 else sh_ref[r, lo:lo + sub_rows, :]
            term = w_ref[j:j + 1, :] * rows
            acc = term if acc is None else acc + term
        o_ref[r0:r0 + sub_rows, :] = acc


def _conv_prompt(z, w_dw, n_seq, seq):
    bt, bd = 512, 256
    t_blocks = seq // bt
    halo_per_seq = seq // CONV_HALO
    halo_per_blk = bt // CONV_HALO
    grid = (n_seq, t_blocks, D_MODEL // bd)
    return pl.pallas_call(
        functools.partial(_conv_prompt_body, sub_rows=32),
        grid=grid,
        in_specs=[
            pl.BlockSpec((bt, bd), lambda n, t, d: (n * t_blocks + t, d)),
            pl.BlockSpec((CONV_HALO, bd),
                         lambda n, t, d: (jnp.maximum(n * halo_per_seq + t * halo_per_blk - 1, 0), d)),
            pl.BlockSpec((CONV_W, bd), lambda n, t, d: (0, d)),
        ],
        out_specs=pl.BlockSpec((bt, bd), lambda n, t, d: (n * t_blocks + t, d)),
        out_shape=jax.ShapeDtypeStruct(z.shape, F32),
        scratch_shapes=[pltpu.VMEM((CONV_HALO + bt, bd), F32),
                        pltpu.VMEM((SUBLANES, CONV_HALO + bt, bd), F32)],
        compiler_params=_params(3), name="conv_prompt")(z, z, w_dw)


def _conv_sample_body(st_ref, z_ref, wt_ref, wz_ref, y_ref, ns_ref):
    st = st_ref[...]
    z = z_ref[...]
    n_old, n_new = st.shape[1], z.shape[1]
    for t in range(n_new):
        y_ref[t] = jnp.sum(st * wt_ref[t][None], axis=1) + jnp.sum(z * wz_ref[t][None], axis=1)
    ns_ref[:, 0:n_old - n_new, :] = st[:, n_new:, :]
    ns_ref[:, n_old - n_new:, :] = z


def _conv_sample(st, z, w_dw):
    _, n_seq, n_old, _ = st.shape
    n_new = z.shape[1]
    bd = 256
    wt = jnp.stack([jnp.pad(w_dw[:n_old - t], ((t, 0), (0, 0))) for t in range(n_new)])
    wz = jnp.stack([jnp.pad(w_dw[n_old - t:], ((0, n_new - 1 - t), (0, 0))) for t in range(n_new)])
    state = pl.BlockSpec((None, n_seq, n_old, bd), lambda d: (0, 0, 0, d))
    taps = lambda rows: pl.BlockSpec((n_new, rows, bd), lambda d: (0, 0, d))
    return pl.pallas_call(
        _conv_sample_body,
        grid=(D_MODEL // bd,),
        in_specs=[state, pl.BlockSpec((n_seq, n_new, bd), lambda d: (0, 0, d)), taps(n_old), taps(n_new)],
        out_specs=[pl.BlockSpec((n_new, n_seq, bd), lambda d: (0, 0, d)), state],
        out_shape=[jax.ShapeDtypeStruct((n_new, n_seq, D_MODEL), F32),
                   jax.ShapeDtypeStruct(st.shape, F32)],
        compiler_params=_params(1), name="conv_sample")(st, z, wt, wz)


def _rel_bucket(dist):
    n = jnp.maximum(dist, 0)
    max_exact = REL_BUCKETS // 2
    nf = jnp.maximum(n, 1).astype(F32)
    large = max_exact + (jnp.log(nf / max_exact) / math.log(REL_MAX_DIST / max_exact)
                         * (REL_BUCKETS - max_exact)).astype(jnp.int32)
    large = jnp.minimum(large, REL_BUCKETS - 1)
    return jnp.where(n < max_exact, n, large)


def _top_blocks_rows(gate, blk_ix, topk):
    cnt = jnp.zeros(gate.shape, jnp.int32)
    for b in range(gate.shape[0]):
        other = gate[b:b + 1, :]
        beats = (other > gate) | ((other == gate) & (b < blk_ix))
        cnt = cnt + beats.astype(jnp.int32)
    return cnt < topk


def _kmean_body(k_ref, o_ref):
    o_ref[...] = jnp.sum(k_ref[...], axis=0, keepdims=True) / MOBA_BLOCK


def _block_means(k, n_blocks):
    return pl.pallas_call(
        _kmean_body, grid=(n_blocks,),
        in_specs=[pl.BlockSpec((MOBA_BLOCK, D_MODEL), lambda b: (b, 0))],
        out_specs=pl.BlockSpec((None, 1, D_MODEL), lambda b: (b, 0, 0)),
        out_shape=jax.ShapeDtypeStruct((n_blocks, 1, D_MODEL), F32),
        compiler_params=_params(1), name="kmean_prompt")(k)


ATTN_HEADS_PER_GROUP = 8
ATTN_HEADS_PER_STEP = 8


def _attn_prompt_body(q_ref, k_ref, vt_ref, km_ref, bias_ref, o_ref, sel_ref):
    blk = MOBA_BLOCK
    hd = HEAD_DIM
    n_blk = km_ref.shape[0]
    n_heads = q_ref.shape[1] // hd
    qb = pl.program_id(2)
    r_own = pl.multiple_of(qb * blk, blk)
    blk_ix = lax.broadcasted_iota(jnp.int32, (n_blk, n_heads * blk), 0)
    valid = blk_ix < qb

    heads = range(n_heads)
    cols = [slice(g * hd, (g + 1) * hd) for g in heads]
    q = [q_ref[:, cols[g]] for g in heads]
    qh = [q[g].astype(BF16) for g in heads]
    s_t = [lax.dot_general(k_ref[pl.ds(r_own, blk), cols[g]], qh[g], _CONTRACT_LAST,
                           preferred_element_type=F32) + bias_ref[g, 0] for g in heads]
    gate = [lax.dot_general(km_ref[:, cols[g]], q[g], _CONTRACT_LAST, precision=lax.Precision.HIGHEST,
                            preferred_element_type=F32) for g in heads]
    gate = jnp.where(valid, jnp.concatenate(gate, axis=1), -jnp.inf)
    sel_ref[...] = (valid & _top_blocks_rows(gate, blk_ix, MOBA_TOPK)).astype(F32)
    m = [jnp.max(s_t[g], axis=0, keepdims=True) for g in heads]
    p = [jnp.exp(s_t[g] - m[g]) for g in heads]
    l = [jnp.sum(p[g], axis=0, keepdims=True) for g in heads]
    acc = [jnp.dot(vt_ref[qb, cols[g], :], p[g].astype(BF16), preferred_element_type=F32)
           for g in heads]
    state = [(m[g], l[g], acc[g]) for g in heads]

    def body(b, state):
        r0 = pl.multiple_of(b * blk, blk)
        tile = jnp.minimum(qb - b, 2)
        new_state = []
        for g0 in range(0, n_heads, ATTN_HEADS_PER_GROUP):
            grp = range(g0, min(g0 + ATTN_HEADS_PER_GROUP, n_heads))
            s_t = {g: lax.dot_general(k_ref[pl.ds(r0, blk), cols[g]], qh[g], _CONTRACT_LAST,
                                      preferred_element_type=F32) for g in grp}
            s_t = {g: jnp.where(sel_ref[pl.ds(b, 1), g * blk:(g + 1) * blk] > 0.0,
                                s_t[g] + bias_ref[g, tile], -jnp.inf) for g in grp}
            m_new = {g: jnp.maximum(state[g][0], jnp.max(s_t[g], axis=0, keepdims=True)) for g in grp}
            alpha = {g: jnp.exp(state[g][0] - m_new[g]) for g in grp}
            p = {g: jnp.exp(s_t[g] - m_new[g]) for g in grp}
            l = {g: alpha[g] * state[g][1] + jnp.sum(p[g], axis=0, keepdims=True) for g in grp}
            pv = {g: jnp.dot(vt_ref[b, cols[g], :], p[g].astype(BF16), preferred_element_type=F32)
                  for g in grp}
            new_state += [(m_new[g], l[g], alpha[g] * state[g][2] + pv[g]) for g in grp]
        return new_state

    state = lax.fori_loop(0, qb, body, state)
    for g in range(n_heads):
        m, l, acc = state[g]
        o_ref[:, g * hd:(g + 1) * hd] = (acc / l).T.astype(o_ref.dtype)


def _attn_prompt(q, k_bf, vt_bf, k_mean, bias_tiles, n_seq, seq):
    n_blk = seq // MOBA_BLOCK
    hps = ATTN_HEADS_PER_STEP
    w = hps * HEAD_DIM
    grid = (n_seq, N_HEADS // hps, n_blk)
    return pl.pallas_call(
        _attn_prompt_body, grid=grid,
        in_specs=[
            pl.BlockSpec((MOBA_BLOCK, w), lambda n, h, b: (n * n_blk + b, h)),
            pl.BlockSpec((seq, w), lambda n, h, b: (n, h)),
            pl.BlockSpec((None, n_blk, w, MOBA_BLOCK), lambda n, h, b: (n, 0, h, 0)),
            pl.BlockSpec((None, n_blk, w), lambda n, h, b: (n, 0, h)),
            pl.BlockSpec((hps, 3, MOBA_BLOCK, MOBA_BLOCK), lambda n, h, b: (h, 0, 0, 0)),
        ],
        out_specs=pl.BlockSpec((MOBA_BLOCK, w), lambda n, h, b: (n * n_blk + b, h)),
        out_shape=jax.ShapeDtypeStruct(q.shape, BF16),
        scratch_shapes=[pltpu.VMEM((n_blk, hps * MOBA_BLOCK), F32)],
        compiler_params=_params(3), name="attn_prompt")(q, k_bf, vt_bf, k_mean, bias_tiles)


SAMPLE_BLOCKS_PER_STEP = 2
PAGES_PER_BLOCK = MOBA_BLOCK // PAGE_SIZE


SAMPLE_RING = 3


def _attn_sample_body(pt_ref, q_ref, kn_ref, vn_ref, ck_hbm, cv_hbm, bp_ref, bo_ref, o_ref,
                      kbuf, vbuf, sem, kblk_ref, m_ref, l_ref, acc_ref, *, n_seq, n_steps):
    n_pg = SAMPLE_BLOCKS_PER_STEP * PAGES_PER_BLOCK
    page_rows = PAGE_SIZE * N_HEADS
    rows = q_ref.shape[0]
    n_new = rows // N_HEADS
    n_full = kblk_ref.shape[0]
    step = pl.program_id(1)
    last = n_steps - 1
    g = pl.program_id(0) * n_steps + step
    total = n_seq * n_steps

    def page_copies(gg):
        slot = gg % SAMPLE_RING
        seq, first = gg // n_steps, (gg % n_steps) * n_pg
        copies = []
        for i in range(n_pg):
            row0 = pl.multiple_of(pt_ref[seq, first + i] * page_rows, page_rows)
            for hbm, buf in ((ck_hbm, kbuf), (cv_hbm, vbuf)):
                copies.append(pltpu.make_async_copy(hbm.at[pl.ds(row0, page_rows)], buf.at[slot, i],
                                                    sem.at[slot]))
        return copies

    @pl.when(g == 0)
    def _prime():
        for gg in range(min(SAMPLE_RING - 1, total)):
            for c in page_copies(gg):
                c.start()

    @pl.when(g + SAMPLE_RING - 1 < total)
    def _prefetch():
        for c in page_copies(g + SAMPLE_RING - 1):
            c.start()

    for c in page_copies(g):
        c.wait()
    slot = g % SAMPLE_RING

    qh = q_ref[...].astype(BF16)
    pages = range(n_pg)
    kp = [kbuf[slot, i] for i in pages]
    s = [lax.dot_general(qh, kp[i].astype(BF16), _CONTRACT_LAST, preferred_element_type=F32)
         + bp_ref[n_pg * step + i] for i in pages]
    page_sum = [jnp.sum(kp[i].reshape(PAGE_SIZE, N_HEADS, HEAD_DIM), axis=0) for i in pages]
    page_max = [jnp.max(s[i], axis=-1, keepdims=True) for i in pages]
    m_blk = [functools.reduce(jnp.maximum, page_max[c * PAGES_PER_BLOCK:(c + 1) * PAGES_PER_BLOCK])
             for c in range(SAMPLE_BLOCKS_PER_STEP)]
    p = [jnp.exp(s[i] - m_blk[i // PAGES_PER_BLOCK]) for i in pages]
    p_sum = [jnp.sum(p[i], axis=-1, keepdims=True) for i in pages]
    pv = [jnp.dot(p[i].astype(BF16), vbuf[slot, i].astype(BF16), preferred_element_type=F32)
          for i in pages]
    for c in range(SAMPLE_BLOCKS_PER_STEP):
        b = SAMPLE_BLOCKS_PER_STEP * step + c
        of_blk = slice(c * PAGES_PER_BLOCK, (c + 1) * PAGES_PER_BLOCK)
        kblk_ref[b] = functools.reduce(jnp.add, page_sum[of_blk])
        m_ref[b] = m_blk[c]
        l_ref[b] = functools.reduce(jnp.add, p_sum[of_blk])
        acc_ref[b] = functools.reduce(jnp.add, pv[of_blk])

    @pl.when(step == last)
    def _combine():
        q = q_ref[...]
        gates = []
        for c in range(n_full):
            k_mean = kblk_ref[c] / MOBA_BLOCK
            k_rows = jnp.concatenate([k_mean] * n_new, axis=0)
            gates.append(jnp.sum(q * k_rows, axis=-1, keepdims=True))
        sel = []
        for c in range(n_full):
            cnt = jnp.zeros((rows, 1), jnp.int32)
            for o in range(n_full):
                beats = gates[o] > gates[c]
                if o < c:
                    beats = beats | (gates[o] == gates[c])
                cnt = cnt + beats.astype(jnp.int32)
            sel.append(cnt < MOBA_TOPK)

        s_own = lax.dot_general(q, kn_ref[...], _CONTRACT_LAST, precision=lax.Precision.HIGHEST,
                                preferred_element_type=F32) + bo_ref[...]
        m = jnp.max(s_own, axis=-1, keepdims=True)
        for c in range(n_full):
            m = jnp.maximum(m, jnp.where(sel[c], m_ref[c], -jnp.inf))
        p_own = jnp.exp(s_own - m)
        l = jnp.sum(p_own, axis=-1, keepdims=True)
        acc = jnp.dot(p_own, vn_ref[...], precision=lax.Precision.HIGHEST,
                      preferred_element_type=F32)
        for c in range(n_full):
            w = jnp.where(sel[c], jnp.exp(m_ref[c] - m), 0.0)
            l = l + w * l_ref[c]
            acc = acc + w * acc_ref[c]
        o_ref[...] = acc / l


def _attn_sample(q, k_new, v_new, cache_k, cache_v, page_table, bias_past, bias_own):
    n_dec, rows, _ = q.shape
    n_pages = page_table.shape[1]
    n_full = n_pages * PAGE_SIZE // MOBA_BLOCK
    n_pg = SAMPLE_BLOCKS_PER_STEP * PAGES_PER_BLOCK
    assert n_pages % n_pg == 0
    n_steps = n_pages // n_pg
    page_rows = PAGE_SIZE * N_HEADS
    seq_spec = pl.BlockSpec((None, rows, HEAD_DIM), lambda n, s, pt: (n, 0, 0))
    cache_spec = pl.BlockSpec(memory_space=pl.ANY)
    grid_spec = pltpu.PrefetchScalarGridSpec(
        num_scalar_prefetch=1,
        grid=(n_dec, n_steps),
        in_specs=[
            seq_spec, seq_spec, seq_spec, cache_spec, cache_spec,
            pl.BlockSpec((n_pages, rows, page_rows), lambda n, j, pt: (0, 0, 0)),
            pl.BlockSpec((rows, rows), lambda n, j, pt: (0, 0)),
        ],
        out_specs=seq_spec,
        scratch_shapes=[
            pltpu.VMEM((SAMPLE_RING, n_pg, page_rows, HEAD_DIM), F32),
            pltpu.VMEM((SAMPLE_RING, n_pg, page_rows, HEAD_DIM), F32),
            pltpu.SemaphoreType.DMA((SAMPLE_RING,)),
            pltpu.VMEM((n_full, N_HEADS, HEAD_DIM), F32),
            pltpu.VMEM((n_full, rows, 1), F32),
            pltpu.VMEM((n_full, rows, 1), F32),
            pltpu.VMEM((n_full, rows, HEAD_DIM), F32),
        ])
    ck = cache_k.reshape(-1, HEAD_DIM)
    cv = cache_v.reshape(-1, HEAD_DIM)
    return pl.pallas_call(
        functools.partial(_attn_sample_body, n_seq=n_dec, n_steps=n_steps), grid_spec=grid_spec,
        out_shape=jax.ShapeDtypeStruct(q.shape, F32),
        compiler_params=_params(2), name="attn_sample")(
            page_table, q, k_new, v_new, ck, cv, bias_past, bias_own)


def _bias_of_dist(rel_bias, dist):
    onehot = _rel_bucket(dist)[..., None] == jnp.arange(REL_BUCKETS)
    return jnp.sum(jnp.where(onehot[..., None], rel_bias, 0.0), axis=-2)


def _toeplitz(w, n):
    h = w.shape[0]
    wp = jnp.concatenate([w, jnp.zeros((h, 1), w.dtype)], axis=1)
    x = jnp.tile(wp, (1, n))[:, :n * (2 * n - 1)].reshape(h, n, 2 * n - 1)
    return x[:, :, n - 1:]


def _prompt_bias_tiles(rel_bias):
    assert REL_MAX_DIST <= MOBA_BLOCK
    n = MOBA_BLOCK
    tiles = []
    for delta in range(3):
        dist = delta * n - (n - 1) + jnp.arange(2 * n - 1)
        w = jnp.where((dist >= 0)[:, None], _bias_of_dist(rel_bias, dist), -jnp.inf)
        tiles.append(_toeplitz(w.T, n))
    return jnp.stack(tiles, axis=1)


def _sample_bias_tables(rel_bias, past, n_new):
    rows = n_new * N_HEADS
    n_pages = past // PAGE_SIZE
    same_head = jnp.eye(N_HEADS, dtype=bool)
    qpos = past + jnp.arange(n_new)
    dist = qpos[:, None] - jnp.arange(past)[None, :]
    bp = _bias_of_dist(rel_bias, dist)
    bp = bp.reshape(n_new, n_pages, PAGE_SIZE, N_HEADS).transpose(1, 0, 3, 2)
    col = jnp.arange(PAGE_SIZE * N_HEADS)
    spread = (col[None, :] // N_HEADS == jnp.arange(PAGE_SIZE)[:, None]).astype(F32)
    bp = jnp.einsum("prk,kc->prc", bp.reshape(n_pages, rows, PAGE_SIZE), spread,
                    precision=lax.Precision.HIGHEST)
    head_ok = (jnp.arange(rows) % N_HEADS)[:, None] == (col % N_HEADS)[None, :]
    bp = jnp.where(head_ok[None], bp, -jnp.inf)
    d_own = jnp.arange(n_new)[:, None] - jnp.arange(n_new)[None, :]
    bo = jnp.where((d_own >= 0)[..., None], _bias_of_dist(rel_bias, d_own), -jnp.inf)
    bo = bo.transpose(0, 2, 1)
    bo = jnp.where(same_head[None, :, None, :], bo[..., None], -jnp.inf)
    return bp, bo.reshape(rows, rows)


def _trunk(x, mods, wts, *, mode, rows_per_seq, bm, conv_fn, attn_fn):
    mod0, mod1, modkv = mods
    M = lambda arr, c: _Mod(arr, c, mode, rows_per_seq)
    d_ff = wts["ffn_gate"].shape[-1]
    bn = 512

    def ffn(xin, layer, mod):
        gu = _fused_mm(xin, [(wts["ffn_gate"], 0, layer), (wts["ffn_up"], 0, layer)], bm=bm, bn=bn,
                       n_cols=d_ff, out_dtypes=[BF16], prologue="rmsmod",
                       pro_args=(wts["norm_ffn_g"][layer], M(mod, 3), M(mod, 4)),
                       epilogue="swiglu", name=f"ffn_up{layer}_{mode}")
        return _fused_mm(gu, [(wts["ffn_down"], 0, layer)], bm=bm, bn=bn, n_cols=D_MODEL,
                         out_dtypes=[F32], epilogue="resid", epi_args=(xin, M(mod, 5)),
                         name=f"ffn_down{layer}_{mode}")

    z = _fused_mm(x, [(wts["conv_in"], 0), (wts["conv_in"], D_MODEL)], bm=bm, bn=bn, n_cols=D_MODEL,
                  out_dtypes=[F32], prologue="rmsmod",
                  pro_args=(wts["norm_mix_g"][0], M(mod0, 0), M(mod0, 1)), epilogue="glu",
                  name=f"conv_in_{mode}")
    yc = conv_fn(z)
    x = _fused_mm(yc, [(wts["conv_out"], 0)], bm=bm, bn=bn, n_cols=D_MODEL, out_dtypes=[F32],
                  prologue="lnsilu", pro_args=(wts["conv_b_dw"], wts["conv_ln_g"], wts["conv_ln_b"]),
                  epilogue="resid", epi_args=(x, M(mod0, 2)), name=f"conv_out_{mode}")
    x = ffn(x, 0, mod0)

    k, k_bf, v, v_bf = _fused_mm(
        x, [(wts["kv_k"], 0), (wts["kv_v"], 0)], bm=bm, bn=bn, n_cols=D_MODEL,
        out_dtypes=[F32, BF16, F32, BF16], prologue="rmsmod",
        pro_args=(wts["norm_kv_g"], M(modkv, 0), M(modkv, 1)), epilogue="kv",
        epi_args=(wts["kv_g_k"],), name=f"kv_{mode}")

    q = _fused_mm(x, [(wts["attn_q"], 0)], bm=bm, bn=bn, n_cols=D_MODEL, out_dtypes=[F32],
                  prologue="rmsmod", pro_args=(wts["norm_mix_g"][1], M(mod1, 0), M(mod1, 1)),
                  epilogue="headnorm", epi_args=(wts["attn_g_q"],), head_scale=HEAD_DIM ** -0.5,
                  name=f"q_{mode}")
    o = attn_fn(q, k, v, k_bf, v_bf)
    x = _fused_mm(o, [(wts["attn_o"], 0)], bm=bm, bn=bn, n_cols=D_MODEL, out_dtypes=[F32],
                  epilogue="resid", epi_args=(x, M(mod1, 2)), name=f"attn_o_{mode}")
    x = ffn(x, 1, mod1)
    return x, k, v, z


def kernel(x_prompt, x_sample, c_prompt, c_sample, cache_k, cache_v, state_conv, page_table, ada_w, ada_b, norm_mix_g, norm_ffn_g, conv_w_in, conv_w_dw, conv_b_dw, conv_ln_g, conv_ln_b, conv_w_out, kv_ada_w, kv_ada_b, norm_kv_g, kv_w_k, kv_w_v, kv_g_k, attn_w_q, attn_g_q, attn_w_o, rel_bias, ffn_w_gate, ffn_w_up, ffn_w_down):
    n_seq, seq, d = x_prompt.shape
    n_dec, n_new, _ = x_sample.shape
    assert d == D_MODEL and ada_w.shape[0] == 2 and state_conv.shape[0] == 1
    past = page_table.shape[1] * PAGE_SIZE
    assert past % MOBA_BLOCK == 0 and seq % MOBA_BLOCK == 0

    n_tok = n_dec * n_new
    c_rows = -(-(n_tok + n_seq) // 16) * 16
    c_all = jnp.concatenate([jnp.repeat(c_sample, n_new, axis=0), c_prompt,
                             jnp.zeros((c_rows - n_tok - n_seq, d), F32)], axis=0)

    def ada(w, b, name):
        return _fused_mm(c_all, [w], bm=c_rows, bn=512, n_cols=w[0].shape[-1], out_dtypes=[F32],
                         epilogue="bias", epi_args=(b,), name=name)

    mods = [ada((ada_w, 0, 0), ada_b[0], "ada0"), ada((ada_w, 0, 1), ada_b[1], "ada1"),
            ada((kv_ada_w, 0), kv_ada_b, "ada_kv")]
    mods_p = [m[n_tok:n_tok + n_seq].reshape(n_seq, 1, -1) for m in mods]
    mods_s = mods

    wts = {
        "norm_mix_g": norm_mix_g, "norm_ffn_g": norm_ffn_g,
        "conv_in": conv_w_in[0].astype(BF16), "conv_out": conv_w_out[0].astype(BF16),
        "conv_b_dw": conv_b_dw[0], "conv_ln_g": conv_ln_g[0], "conv_ln_b": conv_ln_b[0],
        "norm_kv_g": norm_kv_g, "kv_k": kv_w_k.astype(BF16), "kv_v": kv_w_v.astype(BF16),
        "kv_g_k": kv_g_k, "attn_q": attn_w_q[0].astype(BF16), "attn_g_q": attn_g_q[0],
        "attn_o": attn_w_o[0].astype(BF16),
        "ffn_gate": ffn_w_gate.astype(BF16), "ffn_up": ffn_w_up.astype(BF16),
        "ffn_down": ffn_w_down.astype(BF16),
    }
    w_dw = conv_w_dw[0]

    bias_tiles = _prompt_bias_tiles(rel_bias)

    def attn_prompt(q, k, v, k_bf, v_bf):
        n_blk = seq // MOBA_BLOCK
        k_mean = _block_means(k, n_seq * n_blk).reshape(n_seq, n_blk, D_MODEL)
        vt_bf = v_bf.reshape(n_seq, n_blk, MOBA_BLOCK, D_MODEL).transpose(0, 1, 3, 2)
        return _attn_prompt(q, k_bf, vt_bf, k_mean, bias_tiles, n_seq, seq)

    y_p, k_p, v_p, z_p = _trunk(
        x_prompt.reshape(n_seq * seq, d), mods_p, wts, mode="seq", rows_per_seq=seq, bm=1024,
        conv_fn=lambda z: _conv_prompt(z, w_dw, n_seq, seq), attn_fn=attn_prompt)

    bias_past, bias_own = _sample_bias_tables(rel_bias, past, n_new)
    new_state = []

    def conv_sample(z):
        y_t, ns = _conv_sample(state_conv, z.reshape(n_dec, n_new, d), w_dw)
        new_state.append(ns)
        return y_t.transpose(1, 0, 2).reshape(n_dec * n_new, d)

    def attn_sample(q, k, v, k_bf, v_bf):
        shp = (n_dec, n_new * N_HEADS, HEAD_DIM)
        o = _attn_sample(q.reshape(shp), k.reshape(shp), v.reshape(shp), cache_k, cache_v,
                         page_table, bias_past, bias_own)
        return o.reshape(n_dec * n_new, d)

    y_s, k_s, v_s, z_s = _trunk(
        x_sample.reshape(n_dec * n_new, d), mods_s, wts, mode="row", rows_per_seq=n_new,
        bm=n_dec * n_new, conv_fn=conv_sample, attn_fn=attn_sample)

    heads = (N_HEADS, HEAD_DIM)
    conv_p = z_p.reshape(n_seq, seq, d)[:, seq - (CONV_W - 1):][None]
    conv_s = new_state[0]
    return (y_p.reshape(n_seq, seq, d), y_s.reshape(n_dec, n_new, d),
            k_p.reshape(n_seq, seq, *heads), v_p.reshape(n_seq, seq, *heads),
            k_s.reshape(n_dec, n_new, *heads), v_s.reshape(n_dec, n_new, *heads),
            conv_p, conv_s)
```

```python
import functools
import math

import jax
import jax.numpy as jnp
from jax import lax
from jax.experimental import pallas as pl
from jax.experimental.pallas import tpu as pltpu

F32 = jnp.float32
BF16 = jnp.bfloat16

D_MODEL = 2048
HEAD_DIM = 128
N_HEADS = D_MODEL // HEAD_DIM
CONV_W = 31
MOBA_BLOCK = 256
MOBA_TOPK = 3
REL_BUCKETS = 32
REL_MAX_DIST = 128
PAGE_SIZE = 128
NORM_EPS = 1e-6

SUBLANES = 8
LANES = 128
CONV_HALO = 32
VMEM_LIMIT_BYTES = 56 * 1024 * 1024

_CONTRACT_LAST = (((1,), (1,)), ((), ()))


def _params(n_axes):
    return pltpu.CompilerParams(dimension_semantics=("arbitrary",) * n_axes,
                                vmem_limit_bytes=VMEM_LIMIT_BYTES)


def _sigmoid(x):
    return 1.0 / (1.0 + jnp.exp(-x))


def _head_rms(a, gain, scale):
    outs = []
    for c in range(a.shape[1] // HEAD_DIM):
        blk = a[:, c * HEAD_DIM:(c + 1) * HEAD_DIM]
        ms = jnp.mean(blk * blk, axis=-1, keepdims=True)
        y = blk * lax.rsqrt(ms + NORM_EPS) * gain
        if scale != 1.0:
            y = y * scale
        outs.append(y)
    return jnp.concatenate(outs, axis=1)


def _fused_mm_body(*refs, prologue, epilogue, n_w, n_out, head_scale, row_chunk):
    refs = list(refs)
    h_ref = refs.pop() if prologue is not None else None
    outs = refs[len(refs) - n_out:]
    it = iter(refs[:len(refs) - n_out])
    x_ref = next(it)
    if prologue == "rmsmod":
        g_ref, sh_ref, sc_ref = next(it), next(it), next(it)
    elif prologue == "lnsilu":
        cb_ref, lg_ref, lb_ref = next(it), next(it), next(it)
    w_refs = [next(it) for _ in range(n_w)]
    if epilogue == "bias":
        b_ref = next(it)
    elif epilogue == "resid":
        res_ref, gate_ref = next(it), next(it)
    elif epilogue in ("headnorm", "kv"):
        hg_ref = next(it)

    bm = x_ref.shape[0]

    if prologue is not None:
        @pl.when(pl.program_id(1) == 0)
        def _prologue():
            def vec(ref, r0):
                return ref[...] if ref.shape[0] == 1 else ref[pl.ds(r0, row_chunk), :]

            def chunk(c, carry):
                r0 = pl.multiple_of(c * row_chunk, row_chunk)
                x = x_ref[pl.ds(r0, row_chunk), :].astype(F32)
                if prologue == "rmsmod":
                    ms = jnp.mean(x * x, axis=-1, keepdims=True)
                    y = x * lax.rsqrt(ms + NORM_EPS) * g_ref[...]
                    h = y * (1.0 + vec(sc_ref, r0)) + vec(sh_ref, r0)
                else:
                    x = x + cb_ref[...]
                    mu = jnp.mean(x, axis=-1, keepdims=True)
                    xc = x - mu
                    y = xc * lax.rsqrt(jnp.mean(xc * xc, axis=-1, keepdims=True) + NORM_EPS)
                    y = y * lg_ref[...] + lb_ref[...]
                    h = y * _sigmoid(y)
                h_ref[pl.ds(r0, row_chunk), :] = h.astype(BF16)
                return carry

            lax.fori_loop(0, bm // row_chunk, chunk, 0)

        h = h_ref[...]
    else:
        h = x_ref[...].astype(BF16)

    accs = [jnp.dot(h, w[...].astype(BF16), preferred_element_type=F32) for w in w_refs]

    if epilogue == "bias":
        res = [accs[0] + b_ref[...]]
    elif epilogue == "glu":
        res = [accs[0] * _sigmoid(accs[1])]
    elif epilogue == "swiglu":
        res = [accs[0] * _sigmoid(accs[0]) * accs[1]]
    elif epilogue == "resid":
        res = [res_ref[...] + gate_ref[...] * accs[0]]
    elif epilogue == "headnorm":
        res = [_head_rms(accs[0], hg_ref[...], head_scale)]
    elif epilogue == "kv":
        k = _head_rms(accs[0], hg_ref[...], head_scale)
        res = [k, k, accs[1], accs[1]]
    else:
        res = [accs[0]]
    if len(res) == 1:
        res = res * n_out
    for o_ref, r in zip(outs, res):
        o_ref[...] = r.astype(o_ref.dtype)


class _Mod:
    def __init__(self, arr, chunk, mode, rows_per_seq):
        self.arr, self.chunk, self.mode, self.rows_per_seq = arr, chunk, mode, rows_per_seq

    def spec(self, bm, width, follow_n):
        per = D_MODEL // width
        base = self.chunk * per
        if self.mode == "seq":
            blocks_per_seq = self.rows_per_seq // bm
            return pl.BlockSpec(
                (None, 1, width),
                lambda i, j: (i // blocks_per_seq, 0, base + (j if follow_n else 0)))
        return pl.BlockSpec((bm, width), lambda i, j: (i, base + (j if follow_n else 0)))


def _fused_mm(x, ws, *, bm, bn, n_cols, out_dtypes, prologue=None, pro_args=(), epilogue=None,
              epi_args=(), head_scale=1.0, name):
    m, k = x.shape
    assert m % bm == 0 and n_cols % bn == 0
    grid = (m // bm, n_cols // bn)
    row_chunk = 128 if bm % 128 == 0 else bm

    in_specs = [pl.BlockSpec((bm, k), lambda i, j: (i, 0))]
    args = [x]
    full_vec = pl.BlockSpec((1, k), lambda i, j: (0, 0))
    if prologue == "rmsmod":
        g, sh, sc = pro_args
        in_specs += [full_vec, sh.spec(bm, k, False), sc.spec(bm, k, False)]
        args += [g.reshape(1, k), sh.arr, sc.arr]
    elif prologue == "lnsilu":
        in_specs += [full_vec] * 3
        args += [a.reshape(1, k) for a in pro_args]
    for w, col0, *layer in ws:
        assert col0 % bn == 0
        off = col0 // bn
        if layer:
            in_specs.append(pl.BlockSpec((None, k, bn), lambda i, j, off=off, l=layer[0]: (l, 0, off + j)))
        else:
            in_specs.append(pl.BlockSpec((k, bn), lambda i, j, off=off: (0, off + j)))
        args.append(w)
    if epilogue == "bias":
        in_specs.append(pl.BlockSpec((1, bn), lambda i, j: (0, j)))
        args.append(epi_args[0].reshape(1, n_cols))
    elif epilogue == "resid":
        res, gate = epi_args
        in_specs += [pl.BlockSpec((bm, bn), lambda i, j: (i, j)), gate.spec(bm, bn, True)]
        args += [res, gate.arr]
    elif epilogue in ("headnorm", "kv"):
        in_specs.append(pl.BlockSpec((1, HEAD_DIM), lambda i, j: (0, 0)))
        args.append(epi_args[0].reshape(1, HEAD_DIM))

    out_shape = [jax.ShapeDtypeStruct((m, n_cols), dt) for dt in out_dtypes]
    out_specs = [pl.BlockSpec((bm, bn), lambda i, j: (i, j)) for _ in out_dtypes]
    scratch = [pltpu.VMEM((bm, k), BF16)] if prologue is not None else []
    body = functools.partial(_fused_mm_body, prologue=prologue, epilogue=epilogue, n_w=len(ws),
                             n_out=len(out_dtypes), head_scale=head_scale, row_chunk=row_chunk)
    out = pl.pallas_call(
        body, grid=grid, in_specs=in_specs, out_specs=out_specs, out_shape=out_shape,
        scratch_shapes=scratch, compiler_params=_params(2), name=name)(*args)
    return out[0] if len(out) == 1 else out


def _conv_prompt_body(z_ref, halo_ref, w_ref, o_ref, ext_ref, sh_ref, *, sub_rows):
    bt = z_ref.shape[0]
    tb = pl.program_id(1)

    @pl.when(tb == 0)
    def _():
        ext_ref[0:CONV_HALO, :] = jnp.zeros((CONV_HALO, ext_ref.shape[1]), F32)

    @pl.when(tb > 0)
    def _():
        ext_ref[0:CONV_HALO, :] = halo_ref[...]

    ext_ref[CONV_HALO:, :] = z_ref[...]
    n_sh = bt + CONV_HALO - SUBLANES
    for r in range(1, SUBLANES):
        sh_ref[r, 0:n_sh, :] = ext_ref[r:r + n_sh, :]
    first = CONV_HALO - (CONV_W - 1)
    for r0 in range(0, bt, sub_rows):
        acc = None
        for j in range(CONV_W):
            a, r = divmod(first + j, SUBLANES)
            lo = a * SUBLANES + r0
            rows = ext_ref[lo:lo + sub_rows, :] if r == 0 else sh_ref[r, lo:lo + sub_rows, :]
            term = w_ref[j:j + 1, :] * rows
            acc = term if acc is None else acc + term
        o_ref[r0:r0 + sub_rows, :] = acc


def _conv_prompt(z, w_dw, n_seq, seq):
    bt, bd = 512, 256
    t_blocks = seq // bt
    halo_per_seq = seq // CONV_HALO
    halo_per_blk = bt // CONV_HALO
    grid = (n_seq, t_blocks, D_MODEL // bd)
    return pl.pallas_call(
        functools.partial(_conv_prompt_body, sub_rows=32),
        grid=grid,
        in_specs=[
            pl.BlockSpec((bt, bd), lambda n, t, d: (n * t_blocks + t, d)),
            pl.BlockSpec((CONV_HALO, bd),
                         lambda n, t, d: (jnp.maximum(n * halo_per_seq + t * halo_per_blk - 1, 0), d)),
            pl.BlockSpec((CONV_W, bd), lambda n, t, d: (0, d)),
        ],
        out_specs=pl.BlockSpec((bt, bd), lambda n, t, d: (n * t_blocks + t, d)),
        out_shape=jax.ShapeDtypeStruct(z.shape, F32),
        scratch_shapes=[pltpu.VMEM((CONV_HALO + bt, bd), F32),
                        pltpu.VMEM((SUBLANES, CONV_HALO + bt, bd), F32)],
        compiler_params=_params(3), name="conv_prompt")(z, z, w_dw)


def _conv_sample_body(st_ref, z_ref, wt_ref, wz_ref, y_ref, ns_ref):
    st = st_ref[...]
    z = z_ref[...]
    n_old, n_new = st.shape[1], z.shape[1]
    for t in range(n_new):
        y_ref[t] = jnp.sum(st * wt_ref[t][None], axis=1) + jnp.sum(z * wz_ref[t][None], axis=1)
    ns_ref[:, 0:n_old - n_new, :] = st[:, n_new:, :]
    ns_ref[:, n_old - n_new:, :] = z


def _conv_sample(st, z, w_dw):
    _, n_seq, n_old, _ = st.shape
    n_new = z.shape[1]
    bd = 256
    wt = jnp.stack([jnp.pad(w_dw[:n_old - t], ((t, 0), (0, 0))) for t in range(n_new)])
    wz = jnp.stack([jnp.pad(w_dw[n_old - t:], ((0, n_new - 1 - t), (0, 0))) for t in range(n_new)])
    state = pl.BlockSpec((None, n_seq, n_old, bd), lambda d: (0, 0, 0, d))
    taps = lambda rows: pl.BlockSpec((n_new, rows, bd), lambda d: (0, 0, d))
    return pl.pallas_call(
        _conv_sample_body,
        grid=(D_MODEL // bd,),
        in_specs=[state, pl.BlockSpec((n_seq, n_new, bd), lambda d: (0, 0, d)), taps(n_old), taps(n_new)],
        out_specs=[pl.BlockSpec((n_new, n_seq, bd), lambda d: (0, 0, d)), state],
        out_shape=[jax.ShapeDtypeStruct((n_new, n_seq, D_MODEL), F32),
                   jax.ShapeDtypeStruct(st.shape, F32)],
        compiler_params=_params(1), name="conv_sample")(st, z, wt, wz)


def _rel_bucket(dist):
    n = jnp.maximum(dist, 0)
    max_exact = REL_BUCKETS // 2
    nf = jnp.maximum(n, 1).astype(F32)
    large = max_exact + (jnp.log(nf / max_exact) / math.log(REL_MAX_DIST / max_exact)
                         * (REL_BUCKETS - max_exact)).astype(jnp.int32)
    large = jnp.minimum(large, REL_BUCKETS - 1)
    return jnp.where(n < max_exact, n, large)


def _top_blocks_rows(gate, blk_ix, topk):
    cnt = jnp.zeros(gate.shape, jnp.int32)
    for b in range(gate.shape[0]):
        other = gate[b:b + 1, :]
        beats = (other > gate) | ((other == gate) & (b < blk_ix))
        cnt = cnt + beats.astype(jnp.int32)
    return cnt < topk


def _kmean_body(k_ref, o_ref):
    o_ref[...] = jnp.sum(k_ref[...], axis=0, keepdims=True) / MOBA_BLOCK


def _block_means(k, n_blocks):
    return pl.pallas_call(
        _kmean_body, grid=(n_blocks,),
        in_specs=[pl.BlockSpec((MOBA_BLOCK, D_MODEL), lambda b: (b, 0))],
        out_specs=pl.BlockSpec((None, 1, D_MODEL), lambda b: (b, 0, 0)),
        out_shape=jax.ShapeDtypeStruct((n_blocks, 1, D_MODEL), F32),
        compiler_params=_params(1), name="kmean_prompt")(k)


ATTN_HEADS_PER_GROUP = 8
ATTN_HEADS_PER_STEP = 8


def _attn_prompt_body(q_ref, k_ref, vt_ref, km_ref, bias_ref, o_ref, sel_ref):
    blk = MOBA_BLOCK
    hd = HEAD_DIM
    n_blk = km_ref.shape[0]
    n_heads = q_ref.shape[1] // hd
    qb = pl.program_id(2)
    r_own = pl.multiple_of(qb * blk, blk)
    blk_ix = lax.broadcasted_iota(jnp.int32, (n_blk, n_heads * blk), 0)
    valid = blk_ix < qb

    heads = range(n_heads)
    cols = [slice(g * hd, (g + 1) * hd) for g in heads]
    q = [q_ref[:, cols[g]] for g in heads]
    qh = [q[g].astype(BF16) for g in heads]
    s_t = [lax.dot_general(k_ref[pl.ds(r_own, blk), cols[g]], qh[g], _CONTRACT_LAST,
                           preferred_element_type=F32) + bias_ref[g, 0] for g in heads]
    gate = [lax.dot_general(km_ref[:, cols[g]], q[g], _CONTRACT_LAST, precision=lax.Precision.HIGHEST,
                            preferred_element_type=F32) for g in heads]
    gate = jnp.where(valid, jnp.concatenate(gate, axis=1), -jnp.inf)
    sel_ref[...] = (valid & _top_blocks_rows(gate, blk_ix, MOBA_TOPK)).astype(F32)
    m = [jnp.max(s_t[g], axis=0, keepdims=True) for g in heads]
    p = [jnp.exp(s_t[g] - m[g]) for g in heads]
    l = [jnp.sum(p[g], axis=0, keepdims=True) for g in heads]
    acc = [jnp.dot(vt_ref[qb, cols[g], :], p[g].astype(BF16), preferred_element_type=F32)
           for g in heads]
    state = [(m[g], l[g], acc[g]) for g in heads]

    def body(b, state):
        r0 = pl.multiple_of(b * blk, blk)
        tile = jnp.minimum(qb - b, 2)
        new_state = []
        for g0 in range(0, n_heads, ATTN_HEADS_PER_GROUP):
            grp = range(g0, min(g0 + ATTN_HEADS_PER_GROUP, n_heads))
            s_t = {g: lax.dot_general(k_ref[pl.ds(r0, blk), cols[g]], qh[g], _CONTRACT_LAST,
                                      preferred_element_type=F32) for g in grp}
            s_t = {g: jnp.where(sel_ref[pl.ds(b, 1), g * blk:(g + 1) * blk] > 0.0,
                                s_t[g] + bias_ref[g, tile], -jnp.inf) for g in grp}
            m_new = {g: jnp.maximum(state[g][0], jnp.max(s_t[g], axis=0, keepdims=True)) for g in grp}
            alpha = {g: jnp.exp(state[g][0] - m_new[g]) for g in grp}
            p = {g: jnp.exp(s_t[g] - m_new[g]) for g in grp}
            l = {g: alpha[g] * state[g][1] + jnp.sum(p[g], axis=0, keepdims=True) for g in grp}
            pv = {g: jnp.dot(vt_ref[b, cols[g], :], p[g].astype(BF16), preferred_element_type=F32)
                  for g in grp}
            new_state += [(m_new[g], l[g], alpha[g] * state[g][2] + pv[g]) for g in grp]
        return new_state

    state = lax.fori_loop(0, qb, body, state)
    for g in range(n_heads):
        m, l, acc = state[g]
        o_ref[:, g * hd:(g + 1) * hd] = (acc / l).T.astype(o_ref.dtype)


def _attn_prompt(q, k_bf, vt_bf, k_mean, bias_tiles, n_seq, seq):
    n_blk = seq // MOBA_BLOCK
    hps = ATTN_HEADS_PER_STEP
    w = hps * HEAD_DIM
    grid = (n_seq, N_HEADS // hps, n_blk)
    return pl.pallas_call(
        _attn_prompt_body, grid=grid,
        in_specs=[
            pl.BlockSpec((MOBA_BLOCK, w), lambda n, h, b: (n * n_blk + b, h)),
            pl.BlockSpec((seq, w), lambda n, h, b: (n, h)),
            pl.BlockSpec((None, n_blk, w, MOBA_BLOCK), lambda n, h, b: (n, 0, h, 0)),
            pl.BlockSpec((None, n_blk, w), lambda n, h, b: (n, 0, h)),
            pl.BlockSpec((hps, 3, MOBA_BLOCK, MOBA_BLOCK), lambda n, h, b: (h, 0, 0, 0)),
        ],
        out_specs=pl.BlockSpec((MOBA_BLOCK, w), lambda n, h, b: (n * n_blk + b, h)),
        out_shape=jax.ShapeDtypeStruct(q.shape, BF16),
        scratch_shapes=[pltpu.VMEM((n_blk, hps * MOBA_BLOCK), F32)],
        compiler_params=_params(3), name="attn_prompt")(q, k_bf, vt_bf, k_mean, bias_tiles)


SAMPLE_BLOCKS_PER_STEP = 2
PAGES_PER_BLOCK = MOBA_BLOCK // PAGE_SIZE


SAMPLE_RING = 3


def _attn_sample_body(pt_ref, q_ref, kn_ref, vn_ref, ck_hbm, cv_hbm, bp_ref, bo_ref, o_ref,
                      kbuf, vbuf, sem, kblk_ref, m_ref, l_ref, acc_ref, *, n_seq, n_steps):
    n_pg = SAMPLE_BLOCKS_PER_STEP * PAGES_PER_BLOCK
    page_rows = PAGE_SIZE * N_HEADS
    rows = q_ref.shape[0]
    n_new = rows // N_HEADS
    n_full = kblk_ref.shape[0]
    step = pl.program_id(1)
    last = n_steps - 1
    g = pl.program_id(0) * n_steps + step
    total = n_seq * n_steps

    def page_copies(gg):
        slot = gg % SAMPLE_RING
        seq, first = gg // n_steps, (gg % n_steps) * n_pg
        copies = []
        for i in range(n_pg):
            row0 = pl.multiple_of(pt_ref[seq, first + i] * page_rows, page_rows)
            for hbm, buf in ((ck_hbm, kbuf), (cv_hbm, vbuf)):
                copies.append(pltpu.make_async_copy(hbm.at[pl.ds(row0, page_rows)], buf.at[slot, i],
                                                    sem.at[slot]))
        return copies

    @pl.when(g == 0)
    def _prime():
        for gg in range(min(SAMPLE_RING - 1, total)):
            for c in page_copies(gg):
                c.start()

    @pl.when(g + SAMPLE_RING - 1 < total)
    def _prefetch():
        for c in page_copies(g + SAMPLE_RING - 1):
            c.start()

    for c in page_copies(g):
        c.wait()
    slot = g % SAMPLE_RING

    qh = q_ref[...].astype(BF16)
    pages = range(n_pg)
    kp = [kbuf[slot, i] for i in pages]
    s = [lax.dot_general(qh, kp[i].astype(BF16), _CONTRACT_LAST, preferred_element_type=F32)
         + bp_ref[n_pg * step + i] for i in pages]
    page_sum = [jnp.sum(kp[i].reshape(PAGE_SIZE, N_HEADS, HEAD_DIM), axis=0) for i in pages]
    page_max = [jnp.max(s[i], axis=-1, keepdims=True) for i in pages]
    m_blk = [functools.reduce(jnp.maximum, page_max[c * PAGES_PER_BLOCK:(c + 1) * PAGES_PER_BLOCK])
             for c in range(SAMPLE_BLOCKS_PER_STEP)]
    p = [jnp.exp(s[i] - m_blk[i // PAGES_PER_BLOCK]) for i in pages]
    p_sum = [jnp.sum(p[i], axis=-1, keepdims=True) for i in pages]
    pv = [jnp.dot(p[i].astype(BF16), vbuf[slot, i].astype(BF16), preferred_element_type=F32)
          for i in pages]
    for c in range(SAMPLE_BLOCKS_PER_STEP):
        b = SAMPLE_BLOCKS_PER_STEP * step + c
        of_blk = slice(c * PAGES_PER_BLOCK, (c + 1) * PAGES_PER_BLOCK)
        kblk_ref[b] = functools.reduce(jnp.add, page_sum[of_blk])
        m_ref[b] = m_blk[c]
        l_ref[b] = functools.reduce(jnp.add, p_sum[of_blk])
        acc_ref[b] = functools.reduce(jnp.add, pv[of_blk])

    @pl.when(step == last)
    def _combine():
        q = q_ref[...]
        gates = []
        for c in range(n_full):
            k_mean = kblk_ref[c] / MOBA_BLOCK
            k_rows = jnp.concatenate([k_mean] * n_new, axis=0)
            gates.append(jnp.sum(q * k_rows, axis=-1, keepdims=True))
        sel = []
        for c in range(n_full):
            cnt = jnp.zeros((rows, 1), jnp.int32)
            for o in range(n_full):
                beats = gates[o] > gates[c]
                if o < c:
                    beats = beats | (gates[o] == gates[c])
                cnt = cnt + beats.astype(jnp.int32)
            sel.append(cnt < MOBA_TOPK)

        s_own = lax.dot_general(q, kn_ref[...], _CONTRACT_LAST, precision=lax.Precision.HIGHEST,
                                preferred_element_type=F32) + bo_ref[...]
        m = jnp.max(s_own, axis=-1, keepdims=True)
        for c in range(n_full):
            m = jnp.maximum(m, jnp.where(sel[c], m_ref[c], -jnp.inf))
        p_own = jnp.exp(s_own - m)
        l = jnp.sum(p_own, axis=-1, keepdims=True)
        acc = jnp.dot(p_own, vn_ref[...], precision=lax.Precision.HIGHEST,
                      preferred_element_type=F32)
        for c in range(n_full):
            w = jnp.where(sel[c], jnp.exp(m_ref[c] - m), 0.0)
            l = l + w * l_ref[c]
            acc = acc + w * acc_ref[c]
        o_ref[...] = acc / l


def _attn_sample(q, k_new, v_new, cache_k, cache_v, page_table, bias_past, bias_own):
    n_dec, rows, _ = q.shape
    n_pages = page_table.shape[1]
    n_full = n_pages * PAGE_SIZE // MOBA_BLOCK
    n_pg = SAMPLE_BLOCKS_PER_STEP * PAGES_PER_BLOCK
    assert n_pages % n_pg == 0
    n_steps = n_pages // n_pg
    page_rows = PAGE_SIZE * N_HEADS
    seq_spec = pl.BlockSpec((None, rows, HEAD_DIM), lambda n, s, pt: (n, 0, 0))
    cache_spec = pl.BlockSpec(memory_space=pl.ANY)
    grid_spec = pltpu.PrefetchScalarGridSpec(
        num_scalar_prefetch=1,
        grid=(n_dec, n_steps),
        in_specs=[
            seq_spec, seq_spec, seq_spec, cache_spec, cache_spec,
            pl.BlockSpec((n_pages, rows, page_rows), lambda n, j, pt: (0, 0, 0)),
            pl.BlockSpec((rows, rows), lambda n, j, pt: (0, 0)),
        ],
        out_specs=seq_spec,
        scratch_shapes=[
            pltpu.VMEM((SAMPLE_RING, n_pg, page_rows, HEAD_DIM), F32),
            pltpu.VMEM((SAMPLE_RING, n_pg, page_rows, HEAD_DIM), F32),
            pltpu.SemaphoreType.DMA((SAMPLE_RING,)),
            pltpu.VMEM((n_full, N_HEADS, HEAD_DIM), F32),
            pltpu.VMEM((n_full, rows, 1), F32),
            pltpu.VMEM((n_full, rows, 1), F32),
            pltpu.VMEM((n_full, rows, HEAD_DIM), F32),
        ])
    ck = cache_k.reshape(-1, HEAD_DIM)
    cv = cache_v.reshape(-1, HEAD_DIM)
    return pl.pallas_call(
        functools.partial(_attn_sample_body, n_seq=n_dec, n_steps=n_steps), grid_spec=grid_spec,
        out_shape=jax.ShapeDtypeStruct(q.shape, F32),
        compiler_params=_params(2), name="attn_sample")(
            page_table, q, k_new, v_new, ck, cv, bias_past, bias_own)


def _bias_of_dist(rel_bias, dist):
    onehot = _rel_bucket(dist)[..., None] == jnp.arange(REL_BUCKETS)
    return jnp.sum(jnp.where(onehot[..., None], rel_bias, 0.0), axis=-2)


def _toeplitz(w, n):
    h = w.shape[0]
    wp = jnp.concatenate([w, jnp.zeros((h, 1), w.dtype)], axis=1)
    x = jnp.tile(wp, (1, n))[:, :n * (2 * n - 1)].reshape(h, n, 2 * n - 1)
    return x[:, :, n - 1:]


def _prompt_bias_tiles(rel_bias):
    assert REL_MAX_DIST <= MOBA_BLOCK
    n = MOBA_BLOCK
    tiles = []
    for delta in range(3):
        dist = delta * n - (n - 1) + jnp.arange(2 * n - 1)
        w = jnp.where((dist >= 0)[:, None], _bias_of_dist(rel_bias, dist), -jnp.inf)
        tiles.append(_toeplitz(w.T, n))
    return jnp.stack(tiles, axis=1)


def _sample_bias_tables(rel_bias, past, n_new):
    rows = n_new * N_HEADS
    n_pages = past // PAGE_SIZE
    same_head = jnp.eye(N_HEADS, dtype=bool)
    qpos = past + jnp.arange(n_new)
    dist = qpos[:, None] - jnp.arange(past)[None, :]
    bp = _bias_of_dist(rel_bias, dist)
    bp = bp.reshape(n_new, n_pages, PAGE_SIZE, N_HEADS).transpose(1, 0, 3, 2)
    col = jnp.arange(PAGE_SIZE * N_HEADS)
    spread = (col[None, :] // N_HEADS == jnp.arange(PAGE_SIZE)[:, None]).astype(F32)
    bp = jnp.einsum("prk,kc->prc", bp.reshape(n_pages, rows, PAGE_SIZE), spread,
                    precision=lax.Precision.HIGHEST)
    head_ok = (jnp.arange(rows) % N_HEADS)[:, None] == (col % N_HEADS)[None, :]
    bp = jnp.where(head_ok[None], bp, -jnp.inf)
    d_own = jnp.arange(n_new)[:, None] - jnp.arange(n_new)[None, :]
    bo = jnp.where((d_own >= 0)[..., None], _bias_of_dist(rel_bias, d_own), -jnp.inf)
    bo = bo.transpose(0, 2, 1)
    bo = jnp.where(same_head[None, :, None, :], bo[..., None], -jnp.inf)
    return bp, bo.reshape(rows, rows)


def _trunk(x, mods, wts, *, mode, rows_per_seq, bm, conv_fn, attn_fn):
    mod0, mod1, modkv = mods
    M = lambda arr, c: _Mod(arr, c, mode, rows_per_seq)
    d_ff = wts["ffn_gate"].shape[-1]
    wide = mode == "row"
    bn = 1024 if wide else 512
    bn_kv = 512
    bn_ff = max(b for b in range(LANES, (1536 if wide else 512) + 1, LANES) if d_ff % b == 0)

    def ffn(xin, layer, mod):
        gu = _fused_mm(xin, [(wts["ffn_gate"], 0, layer), (wts["ffn_up"], 0, layer)], bm=bm, bn=bn_ff,
                       n_cols=d_ff, out_dtypes=[BF16], prologue="rmsmod",
                       pro_args=(wts["norm_ffn_g"][layer], M(mod, 3), M(mod, 4)),
                       epilogue="swiglu", name=f"ffn_up{layer}_{mode}")
        return _fused_mm(gu, [(wts["ffn_down"], 0, layer)], bm=bm, bn=bn, n_cols=D_MODEL,
                         out_dtypes=[F32], epilogue="resid", epi_args=(xin, M(mod, 5)),
                         name=f"ffn_down{layer}_{mode}")

    z = _fused_mm(x, [(wts["conv_in"], 0), (wts["conv_in"], D_MODEL)], bm=bm, bn=bn, n_cols=D_MODEL,
                  out_dtypes=[F32], prologue="rmsmod",
                  pro_args=(wts["norm_mix_g"][0], M(mod0, 0), M(mod0, 1)), epilogue="glu",
                  name=f"conv_in_{mode}")
    yc = conv_fn(z)
    x = _fused_mm(yc, [(wts["conv_out"], 0)], bm=bm, bn=bn, n_cols=D_MODEL, out_dtypes=[F32],
                  prologue="lnsilu", pro_args=(wts["conv_b_dw"], wts["conv_ln_g"], wts["conv_ln_b"]),
                  epilogue="resid", epi_args=(x, M(mod0, 2)), name=f"conv_out_{mode}")
    x = ffn(x, 0, mod0)

    k, k_bf, v, v_bf = _fused_mm(
        x, [(wts["kv_k"], 0), (wts["kv_v"], 0)], bm=bm, bn=bn_kv, n_cols=D_MODEL,
        out_dtypes=[F32, BF16, F32, BF16], prologue="rmsmod",
        pro_args=(wts["norm_kv_g"], M(modkv, 0), M(modkv, 1)), epilogue="kv",
        epi_args=(wts["kv_g_k"],), name=f"kv_{mode}")

    q = _fused_mm(x, [(wts["attn_q"], 0)], bm=bm, bn=bn, n_cols=D_MODEL, out_dtypes=[F32],
                  prologue="rmsmod", pro_args=(wts["norm_mix_g"][1], M(mod1, 0), M(mod1, 1)),
                  epilogue="headnorm", epi_args=(wts["attn_g_q"],), head_scale=HEAD_DIM ** -0.5,
                  name=f"q_{mode}")
    o = attn_fn(q, k, v, k_bf, v_bf)
    x = _fused_mm(o, [(wts["attn_o"], 0)], bm=bm, bn=bn, n_cols=D_MODEL, out_dtypes=[F32],
                  epilogue="resid", epi_args=(x, M(mod1, 2)), name=f"attn_o_{mode}")
    x = ffn(x, 1, mod1)
    return x, k, v, z


def kernel(x_prompt, x_sample, c_prompt, c_sample, cache_k, cache_v, state_conv, page_table, ada_w, ada_b, norm_mix_g, norm_ffn_g, conv_w_in, conv_w_dw, conv_b_dw, conv_ln_g, conv_ln_b, conv_w_out, kv_ada_w, kv_ada_b, norm_kv_g, kv_w_k, kv_w_v, kv_g_k, attn_w_q, attn_g_q, attn_w_o, rel_bias, ffn_w_gate, ffn_w_up, ffn_w_down):
    n_seq, seq, d = x_prompt.shape
    n_dec, n_new, _ = x_sample.shape
    assert d == D_MODEL and ada_w.shape[0] == 2 and state_conv.shape[0] == 1
    past = page_table.shape[1] * PAGE_SIZE
    assert past % MOBA_BLOCK == 0 and seq % MOBA_BLOCK == 0

    n_tok = n_dec * n_new
    c_rows = -(-(n_tok + n_seq) // 16) * 16
    c_all = jnp.concatenate([jnp.repeat(c_sample, n_new, axis=0), c_prompt,
                             jnp.zeros((c_rows - n_tok - n_seq, d), F32)], axis=0)

    def ada(w, b, name):
        return _fused_mm(c_all, [w], bm=c_rows, bn=512, n_cols=w[0].shape[-1], out_dtypes=[F32],
                         epilogue="bias", epi_args=(b,), name=name)

    mods = [ada((ada_w, 0, 0), ada_b[0], "ada0"), ada((ada_w, 0, 1), ada_b[1], "ada1"),
            ada((kv_ada_w, 0), kv_ada_b, "ada_kv")]
    mods_p = [m[n_tok:n_tok + n_seq].reshape(n_seq, 1, -1) for m in mods]
    mods_s = mods

    wts = {
        "norm_mix_g": norm_mix_g, "norm_ffn_g": norm_ffn_g,
        "conv_in": conv_w_in[0].astype(BF16), "conv_out": conv_w_out[0].astype(BF16),
        "conv_b_dw": conv_b_dw[0], "conv_ln_g": conv_ln_g[0], "conv_ln_b": conv_ln_b[0],
        "norm_kv_g": norm_kv_g, "kv_k": kv_w_k.astype(BF16), "kv_v": kv_w_v.astype(BF16),
        "kv_g_k": kv_g_k, "attn_q": attn_w_q[0].astype(BF16), "attn_g_q": attn_g_q[0],
        "attn_o": attn_w_o[0].astype(BF16),
        "ffn_gate": ffn_w_gate.astype(BF16), "ffn_up": ffn_w_up.astype(BF16),
        "ffn_down": ffn_w_down.astype(BF16),
    }
    w_dw = conv_w_dw[0]

    bias_tiles = _prompt_bias_tiles(rel_bias)

    def attn_prompt(q, k, v, k_bf, v_bf):
        n_blk = seq // MOBA_BLOCK
        k_mean = _block_means(k, n_seq * n_blk).reshape(n_seq, n_blk, D_MODEL)
        vt_bf = v_bf.reshape(n_seq, n_blk, MOBA_BLOCK, D_MODEL).transpose(0, 1, 3, 2)
        return _attn_prompt(q, k_bf, vt_bf, k_mean, bias_tiles, n_seq, seq)

    y_p, k_p, v_p, z_p = _trunk(
        x_prompt.reshape(n_seq * seq, d), mods_p, wts, mode="seq", rows_per_seq=seq, bm=1024,
        conv_fn=lambda z: _conv_prompt(z, w_dw, n_seq, seq), attn_fn=attn_prompt)

    bias_past, bias_own = _sample_bias_tables(rel_bias, past, n_new)
    new_state = []

    def conv_sample(z):
        y_t, ns = _conv_sample(state_conv, z.reshape(n_dec, n_new, d), w_dw)
        new_state.append(ns)
        return y_t.transpose(1, 0, 2).reshape(n_dec * n_new, d)

    def attn_sample(q, k, v, k_bf, v_bf):
        shp = (n_dec, n_new * N_HEADS, HEAD_DIM)
        o = _attn_sample(q.reshape(shp), k.reshape(shp), v.reshape(shp), cache_k, cache_v,
                         page_table, bias_past, bias_own)
        return o.reshape(n_dec * n_new, d)

    y_s, k_s, v_s, z_s = _trunk(
        x_sample.reshape(n_dec * n_new, d), mods_s, wts, mode="row", rows_per_seq=n_new,
        bm=n_dec * n_new, conv_fn=conv_sample, attn_fn=attn_sample)

    heads = (N_HEADS, HEAD_DIM)
    conv_p = z_p.reshape(n_seq, seq, d)[:, seq - (CONV_W - 1):][None]
    conv_s = new_state[0]
    return (y_p.reshape(n_seq, seq, d), y_s.reshape(n_dec, n_new, d),
            k_p.reshape(n_seq, seq, *heads), v_p.reshape(n_seq, seq, *heads),
            k_s.reshape(n_dec, n_new, *heads), v_s.reshape(n_dec, n_new, *heads),
            conv_p, conv_s)
```
